```python
import functools
import jax, jax.numpy as jnp
from jax import lax
import numpy as np

D_MODEL = 2048
BATCH = 2
SEQ = 4096
DEPTH = 2
DEC_BATCH = 32
DEC_SEQ = 4
PAST_LEN = 8192
PAGE_SIZE = 128

MIX = D_MODEL // 2
N_BRANCH = 3
M_HEADS = 4
M_DV = MIX // M_HEADS
M_DK = M_DV // 2
F_HEADS = 8
F_HD = MIX // F_HEADS
S_HD = 64
S_HEADS = MIX // S_HD
S_GROUPS = 4
S_HPG = S_HEADS // S_GROUPS
S_STATE = 128
S_CONV = 4
S_CONV_DIM = MIX + 2 * S_GROUPS * S_STATE
D_FF = 4 * D_MODEL
CHUNK = 128
Q_BLOCK = 128
FOX_F_BIAS_LO = 3.0
FOX_F_BIAS_HI = 10.0
MLSTM_F_BIAS = 3.0
EPS = 1e-6
SPLITS = (M_HEADS * M_DK, M_HEADS * M_DK, MIX, M_HEADS, M_HEADS, MIX,
          MIX, MIX, MIX, F_HEADS,
          MIX, S_CONV_DIM, S_HEADS,
          N_BRANCH * D_MODEL)
SPLIT_POINTS = tuple(int(v) for v in np.cumsum(SPLITS)[:-1])
IN_WIDTH = int(sum(SPLITS))

kernel_name = 'hybrid_mlstm_fox_ssd_decoder_step'


def rms_norm(x, g):
    xf = x.astype(jnp.float32)
    y = xf * lax.rsqrt(jnp.mean(xf * xf, axis=-1, keepdims=True) + EPS)
    return (y * g.astype(jnp.float32)).astype(x.dtype)


def chunk_len(t):
    return CHUNK if t % CHUNK == 0 else t


def mlstm_chunked(q, k, v, i_pre, f_pre, C0, n0, m0):
    Bn, T = q.shape[:2]
    L = chunk_len(T)
    nc = T // L
    f32 = jnp.float32
    to_chunks = lambda a: a.reshape((Bn, nc, L) + a.shape[2:]).swapaxes(0, 1)
    k = k * (M_DK ** -0.5)
    logf = jax.nn.log_sigmoid(f_pre.astype(f32))
    logi = i_pre.astype(f32)
    causal = jnp.tril(jnp.ones((L, L), bool))[None, :, :, None]

    def step(carry, inp):
        C, n, m = carry
        qc, kc, vc, ic, fc = inp
        qc, kc, vc = qc.astype(f32), kc.astype(f32), vc.astype(f32)
        b = jnp.cumsum(fc, axis=1)
        logD = jnp.where(causal, b[:, :, None, :] - b[:, None, :, :] + ic[:, None, :, :], -jnp.inf)
        inter = b + m[:, None, :]
        m_t = jnp.maximum(inter, jnp.max(logD, axis=2))
        w_inter = jnp.exp(inter - m_t)
        s = jnp.einsum('bthd,bshd->btsh', qc, kc) * jnp.exp(logD - m_t[:, :, None, :])
        num = jnp.einsum('btsh,bshv->bthv', s, vc) + w_inter[..., None] * jnp.einsum('bhvd,bthd->bthv', C, qc)
        den = jnp.sum(s, axis=2) + w_inter * jnp.einsum('bhd,bthd->bth', n, qc)
        h = num / jnp.maximum(jnp.abs(den), jnp.exp(-m_t))[..., None]
        bL = b[:, -1, :]
        log_end = bL[:, None, :] - b + ic
        m_new = jnp.maximum(bL + m, jnp.max(log_end, axis=1))
        we = jnp.exp(log_end - m_new[:, None, :])
        ws = jnp.exp(bL + m - m_new)
        C_new = ws[..., None, None] * C + jnp.einsum('bsh,bshv,bshd->bhvd', we, vc, kc)
        n_new = ws[..., None] * n + jnp.einsum('bsh,bshd->bhd', we, kc)
        return (C_new, n_new, m_new), h.astype(v.dtype)

    (C, n, m), hs = lax.scan(step, (C0.astype(f32), n0.astype(f32), m0.astype(f32)),
                             tuple(map(to_chunks, (q, k, v, logi, logf))))
    h = hs.swapaxes(0, 1).reshape(Bn, T, M_HEADS, M_DV)
    return h, C.astype(C0.dtype), n.astype(n0.dtype), m.astype(m0.dtype)


def fox_prompt(q, k, v, logf):
    Bn, T, H, HD = q.shape
    scale = HD ** -0.5
    c = jnp.cumsum(logf, axis=1)
    cT = c.transpose(0, 2, 1)
    nb = T // Q_BLOCK
    qb = q.reshape(Bn, nb, Q_BLOCK, H, HD).swapaxes(0, 1)
    cb = c.reshape(Bn, nb, Q_BLOCK, H).swapaxes(0, 1)
    pos_k = jnp.arange(T)

    def block(args):
        qi, ci, i0 = args
        s = jnp.einsum('bqhd,bkhd->bhqk', qi, k).astype(jnp.float32) * scale
        s = s + ci.transpose(0, 2, 1)[..., None] - cT[:, :, None, :]
        pos_q = i0 + jnp.arange(Q_BLOCK)
        s = jnp.where(pos_k[None, :] <= pos_q[:, None], s, -jnp.inf)
        p = jax.nn.softmax(s, axis=-1).astype(v.dtype)
        return jnp.einsum('bhqk,bkhd->bqhd', p, v)

    out = lax.map(block, (qb, cb, jnp.arange(nb) * Q_BLOCK))
    return out.swapaxes(0, 1).reshape(Bn, T, H, HD)


def fox_sample(q, k, v, logf, k_cache, v_cache, lf_cache, page_table):
    Bd, T, H, HD = q.shape
    scale = HD ** -0.5
    kp = k_cache[page_table].reshape(Bd, -1, H, HD)
    vp = v_cache[page_table].reshape(Bd, -1, H, HD)
    lfp = lf_cache[page_table].reshape(Bd, -1, H).astype(jnp.float32)
    n_past = kp.shape[1]
    r_past = (lax.cumsum(lfp, axis=1, reverse=True) - lfp).transpose(0, 2, 1)
    f_new = jnp.cumsum(logf, axis=1).transpose(0, 2, 1)
    s_past = jnp.einsum('bqhd,bkhd->bhqk', q, kp).astype(jnp.float32) * scale \
        + r_past[:, :, None, :] + f_new[..., None]
    s_new = jnp.einsum('bqhd,bkhd->bhqk', q, k).astype(jnp.float32) * scale \
        + f_new[..., None] - f_new[:, :, None, :]
    s_new = jnp.where(jnp.tril(jnp.ones((T, T), bool)), s_new, -jnp.inf)
    p = jax.nn.softmax(jnp.concatenate([s_past, s_new], axis=-1), axis=-1).astype(v.dtype)
    return jnp.einsum('bhqk,bkhd->bqhd', p[..., :n_past], vp) + jnp.einsum('bhqk,bkhd->bqhd', p[..., n_past:], v)


def causal_conv(u, buf, w, b):
    full = jnp.concatenate([buf.astype(u.dtype), u], axis=1)
    out = lax.conv_general_dilated(full, w[:, None, :].astype(u.dtype), window_strides=(1,), padding='VALID',
                                   dimension_numbers=('NWC', 'WIO', 'NWC'), feature_group_count=u.shape[-1])
    return jax.nn.silu(out + b), full[:, -(S_CONV - 1):]


def ssd_chunked(x, dt, A, Bm, Cm, h0):
    Bn, T = x.shape[:2]
    L = chunk_len(T)
    nc = T // L
    f32 = jnp.float32
    to_chunks = lambda a: a.reshape((Bn, nc, L) + a.shape[2:]).swapaxes(0, 1)
    causal = jnp.tril(jnp.ones((L, L), bool))[None, :, :, None, None]

    def step(h, inp):
        xc, dtc, bc, cc = inp
        xc, bc, cc = xc.astype(f32), bc.astype(f32), cc.astype(f32)
        cum = jnp.cumsum(dtc * A, axis=1)
        seg = jnp.where(causal, cum[:, :, None] - cum[:, None, :], -jnp.inf)
        w = jnp.einsum('btgn,bsgn->btsg', cc, bc)[..., None] * jnp.exp(seg) * dtc[:, None]
        y = jnp.einsum('btsgr,bsgrp->btgrp', w, xc) \
            + jnp.exp(cum)[..., None] * jnp.einsum('btgn,bgrpn->btgrp', cc, h)
        cum_end = cum[:, -1]
        w_end = jnp.exp(cum_end[:, None] - cum) * dtc
        h = jnp.exp(cum_end)[..., None, None] * h + jnp.einsum('bsgr,bsgrp,bsgn->bgrpn', w_end, xc, bc)
        return h, y.astype(x.dtype)

    h, ys = lax.scan(step, h0.astype(f32), tuple(map(to_chunks, (x, dt, Bm, Cm))))
    return ys.swapaxes(0, 1).reshape(x.shape), h.astype(h0.dtype)


def trunk_layer(x, p, mC, mn, mm, s_h, conv_buf, attend):
    Bn, T, _ = x.shape
    h = rms_norm(x, p['ln_mix_pre'])
    z = h @ p['w_in']
    (mq, mk, mv, mi, mf, mo, fq, fk, fv, ff, sz, sxbc, sdt, g) = jnp.split(z, SPLIT_POINTS, axis=-1)
    hm, mC, mn, mm = mlstm_chunked(mq.reshape(Bn, T, M_HEADS, M_DK), mk.reshape(Bn, T, M_HEADS, M_DK),
                                   mv.reshape(Bn, T, M_HEADS, M_DV), mi + p['m_b_i'], mf + p['m_b_f'],
                                   mC, mn, mm)
    hm = rms_norm(hm, p['m_norm'].reshape(M_HEADS, M_DV)).reshape(Bn, T, MIX) * jax.nn.sigmoid(mo)
    flogf = jax.nn.log_sigmoid((ff + p['f_b_f']).astype(jnp.float32))
    fk = fk.reshape(Bn, T, F_HEADS, F_HD)
    fv = fv.reshape(Bn, T, F_HEADS, F_HD)
    hf = attend(fq.reshape(Bn, T, F_HEADS, F_HD), fk, fv, flogf).reshape(Bn, T, MIX)
    xbc, conv_buf = causal_conv(sxbc, conv_buf, p['s_conv_w'], p['s_conv_b'])
    sx, sB, sC = jnp.split(xbc, [MIX, MIX + S_GROUPS * S_STATE], axis=-1)
    sx = sx.reshape(Bn, T, S_GROUPS, S_HPG, S_HD)
    dt = jax.nn.softplus((sdt + p['s_dt_bias']).astype(jnp.float32)).reshape(Bn, T, S_GROUPS, S_HPG)
    A = -jnp.exp(p['s_A_log'].astype(jnp.float32)).reshape(S_GROUPS, S_HPG)
    ys, s_h = ssd_chunked(sx, dt, A, sB.reshape(Bn, T, S_GROUPS, S_STATE), sC.reshape(Bn, T, S_GROUPS, S_STATE),
                          s_h.reshape(Bn, S_GROUPS, S_HPG, S_HD, S_STATE))
    ys = ys + p['s_D'].reshape(S_GROUPS, S_HPG, 1) * sx
    ys = rms_norm(ys.reshape(Bn, T, MIX) * jax.nn.silu(sz), p['s_norm'])
    branches = jnp.stack([hm, hf, ys], axis=2)
    yb = jnp.einsum('btkc,kcd->btkd', branches, p['w_branch'])
    gates = jax.nn.sigmoid(g).reshape(Bn, T, N_BRANCH, D_MODEL)
    mixed = jnp.einsum('btkd,btkd->btd', gates, yb) @ p['w_out']
    x = x + rms_norm(mixed, p['ln_mix_post'])
    u = jnp.square(jax.nn.relu(rms_norm(x, p['ln_mlp_pre']) @ p['w_up']))
    x = x + rms_norm(u @ p['w_down'], p['ln_mlp_post'])
    new_state = (fk, fv, flogf.astype(x.dtype), mC, mn, mm, s_h.reshape(Bn, S_HEADS, S_HD, S_STATE), conv_buf)
    return x, new_state


def setup_inputs(seed: int = 0) -> dict:
    key = jax.random.key(seed)
    ks = list(jax.random.split(key, 40))
    f32 = jnp.float32
    n_pages = PAST_LEN // PAGE_SIZE
    n_pool = (5 * DEC_BATCH * n_pages + 3) // 4

    def normal(shape, scale):
        return scale * jax.random.normal(ks.pop(), shape, f32)

    def uniform(shape, lo, hi):
        return jax.random.uniform(ks.pop(), shape, f32, lo, hi)

    def gain(shape):
        return 1.0 + normal(shape, 0.02)

    fox_bias = jnp.linspace(FOX_F_BIAS_LO, FOX_F_BIAS_HI, F_HEADS, dtype=f32)
    x_prompt = normal((BATCH, SEQ, D_MODEL), 1.0)
    x_sample = normal((DEC_BATCH, DEC_SEQ, D_MODEL), 1.0)
    cache_k = normal((DEPTH, n_pool, PAGE_SIZE, F_HEADS, F_HD), 1.0)
    cache_v = normal((DEPTH, n_pool, PAGE_SIZE, F_HEADS, F_HD), 1.0)
    cache_logf = jax.nn.log_sigmoid(fox_bias + normal((DEPTH, n_pool, PAGE_SIZE, F_HEADS), 0.5))
    page_table = jax.random.permutation(ks.pop(), n_pool)[:DEC_BATCH * n_pages].reshape(DEC_BATCH, n_pages).astype(jnp.int32)
    state_mlstm_C = normal((DEPTH, DEC_BATCH, M_HEADS, M_DV, M_DK), 1.0)
    state_mlstm_n = normal((DEPTH, DEC_BATCH, M_HEADS, M_DK), 1.0)
    state_mlstm_m = normal((DEPTH, DEC_BATCH, M_HEADS), 1.0)
    state_ssd = normal((DEPTH, DEC_BATCH, S_HEADS, S_HD, S_STATE), 0.1)
    state_conv = normal((DEPTH, DEC_BATCH, S_CONV - 1, S_CONV_DIM), 1.0)
    dt0 = jnp.exp(uniform((DEPTH, S_HEADS), float(np.log(1e-3)), float(np.log(1e-1))))
    return {
        'x_prompt': x_prompt, 'x_sample': x_sample,
        'cache_k': cache_k, 'cache_v': cache_v, 'cache_logf': cache_logf, 'page_table': page_table,
        'state_mlstm_C': state_mlstm_C, 'state_mlstm_n': state_mlstm_n, 'state_mlstm_m': state_mlstm_m,
        'state_ssd': state_ssd, 'state_conv': state_conv,
        'w_in': normal((DEPTH, D_MODEL, IN_WIDTH), D_MODEL ** -0.5),
        'm_b_i': normal((DEPTH, M_HEADS), 0.1),
        'm_b_f': MLSTM_F_BIAS + normal((DEPTH, M_HEADS), 0.1),
        'm_norm': gain((DEPTH, MIX)),
        'f_b_f': fox_bias + normal((DEPTH, F_HEADS), 0.1),
        's_conv_w': normal((DEPTH, S_CONV, S_CONV_DIM), S_CONV ** -0.5),
        's_conv_b': normal((DEPTH, S_CONV_DIM), 0.02),
        's_dt_bias': dt0 + jnp.log(-jnp.expm1(-dt0)),
        's_A_log': jnp.log(uniform((DEPTH, S_HEADS), 1.0, 16.0)),
        's_D': 1.0 + normal((DEPTH, S_HEADS), 0.1),
        's_norm': gain((DEPTH, MIX)),
        'w_branch': normal((DEPTH, N_BRANCH, MIX, D_MODEL), MIX ** -0.5),
        'w_out': normal((DEPTH, D_MODEL, D_MODEL), D_MODEL ** -0.5),
        'ln_mix_pre': gain((DEPTH, D_MODEL)),
        'ln_mix_post': gain((DEPTH, D_MODEL)),
        'ln_mlp_pre': gain((DEPTH, D_MODEL)),
        'ln_mlp_post': gain((DEPTH, D_MODEL)),
        'w_up': normal((DEPTH, D_MODEL, D_FF), D_MODEL ** -0.5),
        'w_down': normal((DEPTH, D_FF, D_MODEL), D_FF ** -0.5),
    }


def reference(x_prompt, x_sample, cache_k, cache_v, cache_logf, page_table,
              state_mlstm_C, state_mlstm_n, state_mlstm_m, state_ssd, state_conv,
              w_in, m_b_i, m_b_f, m_norm, f_b_f, s_conv_w, s_conv_b, s_dt_bias, s_A_log, s_D, s_norm,
              w_branch, w_out, ln_mix_pre, ln_mix_post, ln_mlp_pre, ln_mlp_post, w_up, w_down):
    xp, xs = x_prompt, x_sample
    bp = xp.shape[0]
    dtp = xp.dtype
    prompt_out = [[] for _ in range(8)]
    sample_out = [[] for _ in range(8)]
    for l in range(DEPTH):
        p = dict(w_in=w_in[l], m_b_i=m_b_i[l], m_b_f=m_b_f[l], m_norm=m_norm[l], f_b_f=f_b_f[l],
                 s_conv_w=s_conv_w[l], s_conv_b=s_conv_b[l], s_dt_bias=s_dt_bias[l], s_A_log=s_A_log[l],
                 s_D=s_D[l], s_norm=s_norm[l], w_branch=w_branch[l], w_out=w_out[l],
                 ln_mix_pre=ln_mix_pre[l], ln_mix_post=ln_mix_post[l], ln_mlp_pre=ln_mlp_pre[l],
                 ln_mlp_post=ln_mlp_post[l], w_up=w_up[l], w_down=w_down[l])
        xp, st_p = trunk_layer(
            xp, p,
            jnp.zeros((bp, M_HEADS, M_DV, M_DK), dtp), jnp.zeros((bp, M_HEADS, M_DK), dtp),
            jnp.zeros((bp, M_HEADS), dtp), jnp.zeros((bp, S_HEADS, S_HD, S_STATE), dtp),
            jnp.zeros((bp, S_CONV - 1, S_CONV_DIM), dtp), fox_prompt)
        attend_s = functools.partial(fox_sample, k_cache=cache_k[l], v_cache=cache_v[l],
                                     lf_cache=cache_logf[l], page_table=page_table)
        xs, st_s = trunk_layer(xs, p, state_mlstm_C[l], state_mlstm_n[l], state_mlstm_m[l],
                               state_ssd[l], state_conv[l], attend_s)
        for lst, a in zip(prompt_out, st_p):
            lst.append(a)
        for lst, a in zip(sample_out, st_s):
            lst.append(a)
    k_p, v_p, lf_p, C_p, n_p, m_p, ssd_p, conv_p = [jnp.stack(a) for a in prompt_out]
    k_s, v_s, lf_s, C_s, n_s, m_s, ssd_s, conv_s = [jnp.stack(a) for a in sample_out]
    return (xp, xs, k_p, v_p, lf_p, k_s, v_s, lf_s, C_p, n_p, m_p, C_s, n_s, m_s, ssd_p, ssd_s, conv_p, conv_s)
```

```python
import functools

import jax
import jax.numpy as jnp
from jax import lax
from jax.experimental import pallas as pl
from jax.experimental.pallas import tpu as pltpu

F32 = jnp.float32
BF16 = jnp.bfloat16
HI = lax.Precision.HIGHEST
EPS = 1e-6
NEG_INF = float("-inf")
LANES = 128
SUBLANES = 8
CHUNK = 128
VMEM_LIMIT = 56 * 1024 * 1024


def _cp(*sem, vmem=VMEM_LIMIT):
    return pltpu.CompilerParams(dimension_semantics=sem, vmem_limit_bytes=vmem)


def _nt(a, b):
    return lax.dot_general(a, b, (((1,), (1,)), ((), ())), preferred_element_type=F32)


def _mm(a, b):
    return jnp.dot(a, b, preferred_element_type=F32)


def _mm_exact(a, b):
    return jnp.dot(a, b, preferred_element_type=F32, precision=HI)


def _softplus(x):
    return jnp.maximum(x, 0.0) + jnp.log1p(jnp.exp(-jnp.abs(x)))


def _log_sigmoid(x):
    return -_softplus(-x)


def _sigmoid(x):
    return 1.0 / (1.0 + jnp.exp(-x))


def _silu(x):
    return x * _sigmoid(x)


def _rms(x, g):
    return x * lax.rsqrt(jnp.mean(x * x, axis=-1, keepdims=True) + EPS) * g


def _pad_rows(a, rows):
    if a.shape[0] == rows:
        return a
    return jnp.concatenate([a, jnp.zeros((rows - a.shape[0], a.shape[1]), a.dtype)], axis=0)


def _imod(x, n):
    assert n & (n - 1) == 0
    return x & (n - 1)


def _idiv(x, n):
    assert n & (n - 1) == 0
    return x >> (n.bit_length() - 1)


def _pick_col(a, sel):
    return jnp.sum(jnp.where(sel, a, 0.0), axis=1, keepdims=True)


def _pick_row(a, sel):
    return jnp.sum(jnp.where(sel, a, 0.0), axis=0, keepdims=True)


class _Layout:
    def __init__(self, d_model, mix, mh, fh, sh, scd):
        self.d, self.mix, self.mh, self.fh, self.sh, self.scd = d_model, mix, mh, fh, sh, scd
        self.dk = mix // mh // 2
        self.dv = mix // mh
        self.fhd = mix // fh
        src = [("mq", mh * self.dk), ("mk", mh * self.dk), ("mv", mix), ("mi", mh), ("mf", mh), ("mo", mix),
               ("fq", mix), ("fk", mix), ("fv", mix), ("ff", fh), ("sz", mix), ("sxbc", scd), ("sdt", sh),
               ("g", 3 * d_model)]
        self.src = {}
        off = 0
        for name, w in src:
            self.src[name] = (off, w)
            off += w
        self.in_width = off
        order = ["mq", "mk", "mv", "sxbc", "mo", "fq", "fk", "fv", "sz", "g"]
        self.dst = {}
        off = 0
        for name in order:
            self.dst[name] = off
            off += self.src[name][1]
        self.small = off
        self.small_order = ["ff", "mi", "mf", "sdt"]
        self.lane = {}
        l = 0
        for name in self.small_order:
            self.lane[name] = l
            l += self.src[name][1]
        assert l <= LANES
        self.small_used = l
        self.order = order
        self.width = off + LANES

    def pack(self, a):
        parts = [a[..., self.src[n][0]:self.src[n][0] + self.src[n][1]] for n in self.order]
        parts += [a[..., self.src[n][0]:self.src[n][0] + self.src[n][1]] for n in self.small_order]
        pad = LANES - self.small_used
        parts.append(jnp.zeros(a.shape[:-1] + (pad,), a.dtype))
        return jnp.concatenate(parts, axis=-1)


def _inproj_kernel(x_ref, g_ref, w_ref, o_ref, h_ref):
    @pl.when(pl.program_id(1) == 0)
    def _():
        h_ref[...] = _rms(x_ref[...], g_ref[...]).astype(BF16)

    o_ref[...] = _mm(h_ref[...], w_ref[...])


def _inproj(x, g, w, l, tm, tn):
    n, d = x.shape
    width = w.shape[-1]
    return pl.pallas_call(
        _inproj_kernel,
        grid=(n // tm, width // tn),
        in_specs=[pl.BlockSpec((tm, d), lambda i, j: (i, 0)),
                  pl.BlockSpec((None, 1, d), lambda i, j: (l, 0, 0)),
                  pl.BlockSpec((None, d, tn), lambda i, j: (l, 0, j))],
        out_specs=pl.BlockSpec((tm, tn), lambda i, j: (i, j)),
        out_shape=jax.ShapeDtypeStruct((n, width), F32),
        scratch_shapes=[pltpu.VMEM((tm, d), BF16)],
        compiler_params=_cp("parallel", "arbitrary"),
        name="inproj",
    )(x, g, w)


def _mlstm_kernel(q_ref, k_ref, v_ref, og_ref, s_ref, brow_ref, bcol_ref, mn_ref, c0_ref, n0_ref, m0_ref,
                  hm_ref, cout_ref, nout_ref, mout_ref, ct_ref, n_ref, m_ref, *, L, lr, lane_i, lane_f, kscale):
    h = pl.program_id(1)
    c = pl.program_id(2)
    nc = pl.num_programs(2)

    @pl.when(c == 0)
    def _():
        ct_ref[...] = c0_ref[0, 0].T
        n_ref[...] = n0_ref[0, 0]
        m_ref[...] = m0_ref[0, 0]

    q = _pad_rows(q_ref[0], L)
    k = _pad_rows(k_ref[0], L) * kscale
    v = _pad_rows(v_ref[0], L)
    gates = _pad_rows(s_ref[0], L)
    row = lax.broadcasted_iota(jnp.int32, (L, 1), 0)
    col = lax.broadcasted_iota(jnp.int32, (1, L), 1)
    lane = lax.broadcasted_iota(jnp.int32, (1, LANES), 1)
    sub = lax.broadcasted_iota(jnp.int32, (LANES, 1), 0)
    valid_c = row < lr
    valid_r = col < lr
    causal = col <= row
    gb = gates + brow_ref[...]
    gt = gates.T + bcol_ref[...]
    lf = jnp.where(valid_c, _log_sigmoid(gb), 0.0)
    lft = jnp.where(valid_r, _log_sigmoid(gt), 0.0)
    bc_all = _mm_exact(causal.astype(F32), lf)
    br_all = _mm_exact(lft, (row <= col).astype(F32))
    i_c = jnp.where(valid_c, _pick_col(gb, lane == lane_i + h), NEG_INF)
    i_r = jnp.where(valid_r, _pick_row(gt, sub == lane_i + h), NEG_INF)
    b_c = _pick_col(bc_all, lane == lane_f + h)
    b_r = _pick_row(br_all, sub == lane_f + h)

    m_prev = m_ref[...]
    log_d = jnp.where(causal, b_c + (i_r - b_r), NEG_INF)
    inter = b_c + m_prev
    m_t = jnp.maximum(inter, jnp.max(log_d, axis=1, keepdims=True))
    w_inter = jnp.exp(inter - m_t)
    qb = q.astype(BF16)
    kb = k.astype(BF16)
    vb = v.astype(BF16)
    s = _nt(qb, kb) * jnp.exp(log_d - m_t)
    ct = ct_ref[...]
    nvec = n_ref[...]
    num = _mm(s.astype(BF16), vb) + w_inter * _mm(qb, ct.astype(BF16))
    nq = jnp.sum(qb.astype(F32) * nvec.astype(BF16).astype(F32), axis=1, keepdims=True)
    den = jnp.sum(s, axis=1, keepdims=True) + w_inter * nq
    hh = num / jnp.maximum(jnp.abs(den), jnp.exp(-m_t))

    b_last = b_c[L - 1:L, :]
    le_c = b_last - b_c + i_c
    le_r = b_last - b_r + i_r
    m_new = jnp.maximum(b_last + m_prev, jnp.max(le_r, axis=1, keepdims=True))
    we_c = jnp.exp(le_c - m_new)
    we_r = jnp.exp(le_r - m_new)
    ws = jnp.exp(b_last + m_prev - m_new)
    ct_ref[...] = ws * ct + _mm(k.T.astype(BF16), (we_c * v).astype(BF16))
    n_ref[...] = ws * nvec + _mm(jnp.broadcast_to(we_r, (SUBLANES, L)).astype(BF16), kb)[0:1]
    m_ref[...] = m_new

    y = _rms(hh, mn_ref[...]) * _sigmoid(_pad_rows(og_ref[0], L))
    hm_ref[0] = y[:hm_ref.shape[1]].astype(BF16)

    @pl.when(c == nc - 1)
    def _():
        cout_ref[0, 0] = ct_ref[...].T
        nout_ref[0, 0] = n_ref[...]
        mout_ref[0, 0] = m_ref[...]


def _mlstm(z3, lay, lr, bias_row, bias_col, m_norm, l, c0, n0, m0):
    bsz, t, _ = z3.shape
    L = CHUNK
    rb = min(t, L)
    nc = t // rb
    mh, dk, dv = lay.mh, lay.dk, lay.dv
    kern = functools.partial(_mlstm_kernel, L=L, lr=lr, lane_i=lay.lane["mi"], lane_f=lay.lane["mf"],
                             kscale=float(dk) ** -0.5)
    qo, ko, vo, oo, so = (lay.dst["mq"] // dk, lay.dst["mk"] // dk, lay.dst["mv"] // dv, lay.dst["mo"] // dv,
                          lay.small // LANES)
    return pl.pallas_call(
        kern,
        grid=(bsz, mh, nc),
        in_specs=[pl.BlockSpec((1, rb, dk), lambda b, h, c: (b, c, qo + h)),
                  pl.BlockSpec((1, rb, dk), lambda b, h, c: (b, c, ko + h)),
                  pl.BlockSpec((1, rb, dv), lambda b, h, c: (b, c, vo + h)),
                  pl.BlockSpec((1, rb, dv), lambda b, h, c: (b, c, oo + h)),
                  pl.BlockSpec((1, rb, LANES), lambda b, h, c: (b, c, so)),
                  pl.BlockSpec((None, 1, LANES), lambda b, h, c: (l, 0, 0)),
                  pl.BlockSpec((None, LANES, 1), lambda b, h, c: (l, 0, 0)),
                  pl.BlockSpec((None, 1, dv), lambda b, h, c: (l, 0, h)),
                  pl.BlockSpec((1, 1, dv, dk), lambda b, h, c: (b, h, 0, 0)),
                  pl.BlockSpec((1, 1, 1, dk), lambda b, h, c: (b, h, 0, 0)),
                  pl.BlockSpec((1, 1, 1, 1), lambda b, h, c: (b, h, 0, 0))],
        out_specs=[pl.BlockSpec((1, rb, dv), lambda b, h, c: (b, c, h)),
                   pl.BlockSpec((1, 1, dv, dk), lambda b, h, c: (b, h, 0, 0)),
                   pl.BlockSpec((1, 1, 1, dk), lambda b, h, c: (b, h, 0, 0)),
                   pl.BlockSpec((1, 1, 1, 1), lambda b, h, c: (b, h, 0, 0))],
        out_shape=[jax.ShapeDtypeStruct((bsz, t, mh * dv), BF16),
                   jax.ShapeDtypeStruct((bsz, mh, dv, dk), F32),
                   jax.ShapeDtypeStruct((bsz, mh, 1, dk), F32),
                   jax.ShapeDtypeStruct((bsz, mh, 1, 1), F32)],
        scratch_shapes=[pltpu.VMEM((dk, dv), F32), pltpu.VMEM((1, dk), F32), pltpu.VMEM((1, 1), F32)],
        compiler_params=_cp("parallel", "parallel", "arbitrary"),
        name="mlstm",
    )(z3, z3, z3, z3, z3, bias_row, bias_col, m_norm, c0, n0, m0)


def _foxcum_kernel(s_ref, brow_ref, bcol_ref, lf_ref, crow_ref, carry_ref, *, L, lr, nh):
    c = pl.program_id(1)

    @pl.when(c == 0)
    def _():
        carry_ref[...] = jnp.zeros_like(carry_ref)

    gates = _pad_rows(s_ref[0], L)
    row = lax.broadcasted_iota(jnp.int32, (L, 1), 0)
    col = lax.broadcasted_iota(jnp.int32, (1, L), 1)
    lf = _log_sigmoid(gates + brow_ref[...])
    lf_ref[0] = lf[:lf_ref.shape[1], 0:nh]
    lft = jnp.where(col < lr, _log_sigmoid(gates.T + bcol_ref[...]), 0.0)
    csum = _mm_exact(lft[0:nh], (row <= col).astype(F32)) + carry_ref[...]
    crow_ref[0] = csum
    carry_ref[...] = csum[:, L - 1:L]


def _foxcum(z3, lay, lr, bias_row, bias_col, l):
    bsz, t, _ = z3.shape
    L = CHUNK
    rb = min(t, L)
    nc = t // rb
    fh = lay.fh
    so = lay.small // LANES
    return pl.pallas_call(
        functools.partial(_foxcum_kernel, L=L, lr=lr, nh=fh),
        grid=(bsz, nc),
        in_specs=[pl.BlockSpec((1, rb, LANES), lambda b, c: (b, c, so)),
                  pl.BlockSpec((None, 1, LANES), lambda b, c: (l, 0, 0)),
                  pl.BlockSpec((None, LANES, 1), lambda b, c: (l, 0, 0))],
        out_specs=[pl.BlockSpec((1, rb, fh), lambda b, c: (b, c, 0)),
                   pl.BlockSpec((1, fh, L), lambda b, c: (b, 0, c))],
        out_shape=[jax.ShapeDtypeStruct((bsz, t, fh), F32),
                   jax.ShapeDtypeStruct((bsz, fh, nc * L), F32)],
        scratch_shapes=[pltpu.VMEM((fh, 1), F32)],
        compiler_params=_cp("parallel", "arbitrary"),
        name="foxcum",
    )(z3, bias_row, bias_col)


def _flash_kernel(q_ref, k_ref, v_ref, c_ref, o_ref, m_ref, l_ref, acc_ref, *, tq, tk, scale):
    qi = pl.program_id(2)
    ki = pl.program_id(3)
    nk = pl.num_programs(3)

    @pl.when(ki == 0)
    def _():
        m_ref[...] = jnp.full_like(m_ref, NEG_INF)
        l_ref[...] = jnp.zeros_like(l_ref)
        acc_ref[...] = jnp.zeros_like(acc_ref)

    @pl.when(ki * tk <= qi * tq + tq - 1)
    def _():
        s = _nt(q_ref[0].astype(BF16), k_ref[0].astype(BF16)) * scale - c_ref[0, 0]
        rowpos = qi * tq + lax.broadcasted_iota(jnp.int32, (tq, 1), 0)
        colpos = ki * tk + lax.broadcasted_iota(jnp.int32, (1, tk), 1)
        s = jnp.where(colpos <= rowpos, s, NEG_INF)
        m_prev = m_ref[...]
        m_new = jnp.maximum(m_prev, jnp.max(s, axis=1, keepdims=True))
        alpha = jnp.exp(m_prev - m_new)
        p = jnp.exp(s - m_new)
        l_ref[...] = alpha * l_ref[...] + jnp.sum(p, axis=1, keepdims=True)
        acc_ref[...] = alpha * acc_ref[...] + _mm(p.astype(BF16), v_ref[0].astype(BF16))
        m_ref[...] = m_new

    @pl.when(ki == nk - 1)
    def _():
        o_ref[0] = (acc_ref[...] / l_ref[...]).astype(BF16)


def _flash(z3, crow4, lay, tq, tk):
    bsz, t, _ = z3.shape
    fh, hd = lay.fh, lay.fhd
    qo, ko, vo = lay.dst["fq"] // hd, lay.dst["fk"] // hd, lay.dst["fv"] // hd

    def kmap(b, h, qi, ki):
        return jnp.minimum(ki, (qi * tq + tq - 1) // tk)

    return pl.pallas_call(
        functools.partial(_flash_kernel, tq=tq, tk=tk, scale=float(hd) ** -0.5),
        grid=(bsz, fh, t // tq, t // tk),
        in_specs=[pl.BlockSpec((1, tq, hd), lambda b, h, qi, ki: (b, qi, qo + h)),
                  pl.BlockSpec((1, tk, hd), lambda b, h, qi, ki: (b, kmap(b, h, qi, ki), ko + h)),
                  pl.BlockSpec((1, tk, hd), lambda b, h, qi, ki: (b, kmap(b, h, qi, ki), vo + h)),
                  pl.BlockSpec((1, 1, 1, tk), lambda b, h, qi, ki: (b, h, 0, kmap(b, h, qi, ki)))],
        out_specs=pl.BlockSpec((1, tq, hd), lambda b, h, qi, ki: (b, qi, h)),
        out_shape=jax.ShapeDtypeStruct((bsz, t, fh * hd), BF16),
        scratch_shapes=[pltpu.VMEM((tq, 1), F32), pltpu.VMEM((tq, 1), F32), pltpu.VMEM((tq, hd), F32)],
        compiler_params=_cp("parallel", "parallel", "parallel", "arbitrary"),
        name="fox_flash",
    )(z3, z3, z3, crow4)


def _lfsuffix_kernel(x_ref, r_ref, t_ref, msuf_ref, mtot_ref, *, nh):
    @pl.when(pl.program_id(0) == 0)
    def _():
        w = msuf_ref.shape[0]
        ri = lax.broadcasted_iota(jnp.int32, (w, 1), 0)
        ci = lax.broadcasted_iota(jnp.int32, (1, w), 1)
        same = _imod(ri, nh) == _imod(ci, nh)
        msuf_ref[...] = jnp.where(same & (_idiv(ri, nh) > _idiv(ci, nh)), 1.0, 0.0)
        mtot_ref[...] = jnp.where(same, 1.0, 0.0)

    x = x_ref[...]
    r_ref[...] = _mm_exact(x, msuf_ref[...])
    t_ref[...] = _mm_exact(x, mtot_ref[...])


def _lfsuffix(lf_flat, nh, tm):
    n, w = lf_flat.shape
    return pl.pallas_call(
        functools.partial(_lfsuffix_kernel, nh=nh),
        grid=(n // tm,),
        in_specs=[pl.BlockSpec((tm, w), lambda i: (i, 0))],
        out_specs=[pl.BlockSpec((tm, w), lambda i: (i, 0)), pl.BlockSpec((tm, w), lambda i: (i, 0))],
        out_shape=[jax.ShapeDtypeStruct((n, w), F32), jax.ShapeDtypeStruct((n, w), F32)],
        scratch_shapes=[pltpu.VMEM((w, w), F32), pltpu.VMEM((w, w), F32)],
        compiler_params=_cp("arbitrary"),
        name="lf_suffix",
    )(lf_flat)


def _decode_kernel(pt_ref, q_ref, kn_ref, vn_ref, fn_ref, kc_ref, vc_ref, r_ref, t_ref, o_ref,
                   m_ref, l_ref, acc_ref, tail_ref, *, scale, nq, nh):
    j = pl.program_id(1)
    nj = pl.num_programs(1)
    rows = nq * nh
    hd = q_ref.shape[-1]

    @pl.when(j == 0)
    def _():
        m_ref[...] = jnp.full_like(m_ref, NEG_INF)
        l_ref[...] = jnp.zeros_like(l_ref)
        acc_ref[...] = jnp.zeros_like(acc_ref)
        tail_ref[...] = jnp.zeros_like(tail_ref)

    qs = q_ref[0].reshape(rows, hd).astype(BF16)
    row_h = _imod(lax.broadcasted_iota(jnp.int32, (rows, 1), 0), nh)

    def update(s, vb):
        m_prev = m_ref[...]
        m_new = jnp.maximum(m_prev, jnp.max(s, axis=1, keepdims=True))
        alpha = jnp.exp(m_prev - m_new)
        p = jnp.exp(s - m_new)
        l_ref[...] = alpha * l_ref[...] + jnp.sum(p, axis=1, keepdims=True)
        acc_ref[...] = alpha * acc_ref[...] + _mm(p.astype(BF16), vb)
        m_ref[...] = m_new

    w = kc_ref.shape[0] * kc_ref.shape[1]
    kf = kc_ref[...].reshape(w, hd).astype(BF16)
    vf = vc_ref[...].reshape(w, hd).astype(BF16)
    lane_h = _imod(lax.broadcasted_iota(jnp.int32, (1, w), 1), nh)
    s = _nt(qs, kf) * scale + (r_ref[0] + tail_ref[...])
    update(jnp.where(lane_h == row_h, s, NEG_INF), vf)
    tail_ref[...] = tail_ref[...] + t_ref[0]

    @pl.when(j == nj - 1)
    def _():
        kn = _pad_rows(kn_ref[0].reshape(rows, hd), LANES).astype(BF16)
        vn = _pad_rows(vn_ref[0].reshape(rows, hd), LANES).astype(BF16)
        lane = lax.broadcasted_iota(jnp.int32, (1, LANES), 1)
        row_q = _idiv(lax.broadcasted_iota(jnp.int32, (rows, 1), 0), nh)
        ok = (_imod(lane, nh) == row_h) & (_idiv(lane, nh) <= row_q) & (lane < rows)
        s_new = _nt(qs, kn) * scale - fn_ref[0]
        update(jnp.where(ok, s_new, NEG_INF), vn)
        o_ref[0] = (acc_ref[...] / l_ref[...]).reshape(nq, nh, hd)


def _decode(z5, fnew, cache_k, cache_v, rsuf, rtot, page_table, lay, l, nq):
    bsz = z5.shape[0]
    fh, hd = lay.fh, lay.fhd
    n_pool, page = cache_k.shape[1], cache_k.shape[2]
    n_pages = page_table.shape[1]
    w = page * fh
    qo, ko, vo = lay.dst["fq"] // (fh * hd), lay.dst["fk"] // (fh * hd), lay.dst["fv"] // (fh * hd)

    def pg(b, j, pt):
        return pt[b, n_pages - 1 - j]

    grid_spec = pltpu.PrefetchScalarGridSpec(
        num_scalar_prefetch=1,
        grid=(bsz, n_pages),
        in_specs=[pl.BlockSpec((1, nq, fh, hd), lambda b, j, pt: (b, 0, qo, 0)),
                  pl.BlockSpec((1, nq, fh, hd), lambda b, j, pt: (b, 0, ko, 0)),
                  pl.BlockSpec((1, nq, fh, hd), lambda b, j, pt: (b, 0, vo, 0)),
                  pl.BlockSpec((1, 1, LANES), lambda b, j, pt: (b, 0, 0)),
                  pl.BlockSpec((None, None, page, fh, hd), lambda b, j, pt: (l, pg(b, j, pt), 0, 0, 0)),
                  pl.BlockSpec((None, None, page, fh, hd), lambda b, j, pt: (l, pg(b, j, pt), 0, 0, 0)),
                  pl.BlockSpec((1, 1, w), lambda b, j, pt: (l * n_pool + pg(b, j, pt), 0, 0)),
                  pl.BlockSpec((1, 1, w), lambda b, j, pt: (l * n_pool + pg(b, j, pt), 0, 0))],
        out_specs=pl.BlockSpec((1, nq, fh, hd), lambda b, j, pt: (b, 0, 0, 0)),
        scratch_shapes=[pltpu.VMEM((nq * fh, 1), F32), pltpu.VMEM((nq * fh, 1), F32),
                        pltpu.VMEM((nq * fh, hd), F32), pltpu.VMEM((1, w), F32)],
    )
    return pl.pallas_call(
        functools.partial(_decode_kernel, scale=float(hd) ** -0.5, nq=nq, nh=fh),
        grid_spec=grid_spec,
        out_shape=jax.ShapeDtypeStruct((bsz, nq, fh, hd), F32),
        compiler_params=_cp("parallel", "arbitrary"),
        name="fox_decode",
    )(page_table, z5, z5, z5, fnew, cache_k, cache_v, rsuf, rtot)


def _ssd_kernel(u_ref, sz_ref, s_ref, brow_ref, bcol_ref, alrow_ref, alcol_ref, cw_ref, cb_ref, dexp_ref, sn_ref,
                h0_ref, cv0_ref, y_ref, hout_ref, xs_ref, ht_ref, *, L, lr, ng, hpg, hd, ns, lane_dt, kconv):
    c = pl.program_id(1)
    nc = pl.num_programs(1)
    mix = ng * hpg * hd
    gw = hpg * hd

    @pl.when(c == 0)
    def _():
        xs_ref[0:SUBLANES, :] = cv0_ref[0]
        for g in range(ng):
            ht_ref[g] = h0_ref[0, g * gw:(g + 1) * gw, :].T

    @pl.when(c > 0)
    def _():
        xs_ref[0:SUBLANES, :] = xs_ref[L:L + SUBLANES, :]

    xs_ref[SUBLANES:SUBLANES + L, :] = _pad_rows(u_ref[0], L)
    conv = cb_ref[...]
    for jj in range(kconv):
        off = SUBLANES - (kconv - 1) + jj
        conv = conv + cw_ref[jj:jj + 1, :] * xs_ref[pl.ds(off, L), :]
    xbc = _silu(conv)
    x = xbc[:, :mix]

    gates = _pad_rows(s_ref[0], L)
    row = lax.broadcasted_iota(jnp.int32, (L, 1), 0)
    col = lax.broadcasted_iota(jnp.int32, (1, L), 1)
    lane = lax.broadcasted_iota(jnp.int32, (1, LANES), 1)
    causal = col <= row
    dt_c = jnp.where(row < lr, _softplus(gates + brow_ref[...]), 0.0)
    dt_r = jnp.where(col < lr, _softplus(gates.T + bcol_ref[...]), 0.0)
    cum_c = _mm_exact(causal.astype(F32), dt_c * (-jnp.exp(alrow_ref[...])))
    cum_r = _mm_exact(dt_r * (-jnp.exp(alcol_ref[...])), (row <= col).astype(F32))
    lane_grp = _idiv(lax.broadcasted_iota(jnp.int32, (1, gw), 1), hd)

    ys = []
    for g in range(ng):
        bg = xbc[:, mix + g * ns:mix + (g + 1) * ns]
        cg = xbc[:, mix + (ng + g) * ns:mix + (ng + g + 1) * ns]
        bb = bg.astype(BF16)
        cb = cg.astype(BF16)
        cbm = _nt(cb, bb)
        xg = x[:, g * gw:(g + 1) * gw]
        ht = ht_ref[g]
        y = jnp.zeros((L, gw), F32)
        ecum = jnp.zeros((L, gw), F32)
        wend = jnp.zeros((L, gw), F32)
        dec = jnp.zeros((1, gw), F32)
        for r in range(hpg):
            idx = lane_dt + g * hpg + r
            cc = _pick_col(cum_c, lane == idx)
            dtc = _pick_col(dt_c, lane == idx)
            cr = cum_r[idx:idx + 1, :]
            dtr = dt_r[idx:idx + 1, :]
            wmat = cbm * jnp.exp(jnp.where(causal, cc - cr, NEG_INF)) * dtr
            sel = lane_grp == r
            y = y + _mm(wmat.astype(BF16), jnp.where(sel, xg, 0.0).astype(BF16))
            ce = cc[L - 1:L, :]
            ecum = jnp.where(sel, jnp.exp(cc), ecum)
            wend = jnp.where(sel, jnp.exp(ce - cc) * dtc, wend)
            dec = jnp.where(sel, jnp.exp(ce), dec)
        y = y + ecum * _mm(cb, ht.astype(BF16))
        ht_ref[g] = dec * ht + _mm(bg.T.astype(BF16), (wend * xg).astype(BF16))
        ys.append(y + dexp_ref[:, g * gw:(g + 1) * gw] * xg)
    yall = jnp.concatenate(ys, axis=1) * _silu(_pad_rows(sz_ref[0], L))
    y_ref[0] = _rms(yall, sn_ref[...])[:y_ref.shape[1]].astype(BF16)

    @pl.when(c == nc - 1)
    def _():
        for g in range(ng):
            hout_ref[0, g * gw:(g + 1) * gw, :] = ht_ref[g].T


def _ssd(z3, lay, lr, bias_row, bias_col, alog_row, alog_col, conv_w, conv_b, d_exp, s_norm, l, h0, cv0, ns, hd):
    bsz, t, _ = z3.shape
    L = CHUNK
    rb = min(t, L)
    nc = t // rb
    mix, scd, sh = lay.mix, lay.scd, lay.sh
    ng = (scd - mix) // (2 * ns)
    hpg = sh // ng
    kconv = conv_w.shape[1]
    kern = functools.partial(_ssd_kernel, L=L, lr=lr, ng=ng, hpg=hpg, hd=hd, ns=ns, lane_dt=lay.lane["sdt"],
                             kconv=kconv)
    uo, zo, so = lay.dst["sxbc"] // scd, lay.dst["sz"] // mix, lay.small // LANES
    return pl.pallas_call(
        kern,
        grid=(bsz, nc),
        in_specs=[pl.BlockSpec((1, rb, scd), lambda b, c: (b, c, uo)),
                  pl.BlockSpec((1, rb, mix), lambda b, c: (b, c, zo)),
                  pl.BlockSpec((1, rb, LANES), lambda b, c: (b, c, so)),
                  pl.BlockSpec((None, 1, LANES), lambda b, c: (l, 0, 0)),
                  pl.BlockSpec((None, LANES, 1), lambda b, c: (l, 0, 0)),
                  pl.BlockSpec((None, 1, LANES), lambda b, c: (l, 0, 0)),
                  pl.BlockSpec((None, LANES, 1), lambda b, c: (l, 0, 0)),
                  pl.BlockSpec((None, kconv, scd), lambda b, c: (l, 0, 0)),
                  pl.BlockSpec((None, 1, scd), lambda b, c: (l, 0, 0)),
                  pl.BlockSpec((None, 1, mix), lambda b, c: (l, 0, 0)),
                  pl.BlockSpec((None, 1, mix), lambda b, c: (l, 0, 0)),
                  pl.BlockSpec((1, sh * hd, ns), lambda b, c: (b, 0, 0)),
                  pl.BlockSpec((1, SUBLANES, scd), lambda b, c: (b, 0, 0))],
        out_specs=[pl.BlockSpec((1, rb, mix), lambda b, c: (b, c, 0)),
                   pl.BlockSpec((1, sh * hd, ns), lambda b, c: (b, 0, 0))],
        out_shape=[jax.ShapeDtypeStruct((bsz, t, mix), BF16),
                   jax.ShapeDtypeStruct((bsz, sh * hd, ns), F32)],
        scratch_shapes=[pltpu.VMEM((L + 2 * SUBLANES, scd), F32), pltpu.VMEM((ng, ns, hpg * hd), F32)],
        compiler_params=_cp("parallel", "arbitrary"),
        name="ssd",
    )(z3, z3, z3, bias_row, bias_col, alog_row, alog_col, conv_w, conv_b, d_exp, s_norm, h0, cv0)


def _merge_kernel(hm_ref, hf_ref, ys_ref, wb_ref, g0_ref, g1_ref, g2_ref, o_ref):
    acc = _sigmoid(g0_ref[...]) * _mm(hm_ref[...], wb_ref[0])
    acc = acc + _sigmoid(g1_ref[...]) * _mm(hf_ref[...], wb_ref[1])
    acc = acc + _sigmoid(g2_ref[...]) * _mm(ys_ref[...], wb_ref[2])
    o_ref[...] = acc.astype(BF16)


def _merge(hm, hf, ys, wb, z, lay, l, tm, tn):
    n, mix = hm.shape
    d = lay.d
    go = lay.dst["g"] // tn
    per = d // tn
    gspec = lambda k: pl.BlockSpec((tm, tn), lambda i, j: (i, go + k * per + j))
    act = pl.BlockSpec((tm, mix), lambda i, j: (i, 0))
    return pl.pallas_call(
        _merge_kernel,
        grid=(n // tm, d // tn),
        in_specs=[act, act, act, pl.BlockSpec((None, 3, mix, tn), lambda i, j: (l, 0, 0, j)),
                  gspec(0), gspec(1), gspec(2)],
        out_specs=pl.BlockSpec((tm, tn), lambda i, j: (i, j)),
        out_shape=jax.ShapeDtypeStruct((n, d), BF16),
        compiler_params=_cp("parallel", "arbitrary"),
        name="merge",
    )(hm, hf, ys, wb, z, z, z)


def _outproj_kernel(a_ref, w_ref, g_ref, x_ref, o_ref):
    o_ref[...] = x_ref[...] + _rms(_mm(a_ref[...], w_ref[...]), g_ref[...])


def _outproj(a, w, g, x, l, tm):
    n, d = x.shape
    return pl.pallas_call(
        _outproj_kernel,
        grid=(n // tm,),
        in_specs=[pl.BlockSpec((tm, d), lambda i: (i, 0)),
                  pl.BlockSpec((None, d, d), lambda i: (l, 0, 0)),
                  pl.BlockSpec((None, 1, d), lambda i: (l, 0, 0)),
                  pl.BlockSpec((tm, d), lambda i: (i, 0))],
        out_specs=pl.BlockSpec((tm, d), lambda i: (i, 0)),
        out_shape=jax.ShapeDtypeStruct((n, d), F32),
        compiler_params=_cp("parallel"),
        name="outproj",
    )(a, w, g, x)


def _mlp_kernel(x_ref, gpre_ref, wu_ref, wd_ref, gpost_ref, o_ref, h_ref, acc_ref):
    j = pl.program_id(1)

    @pl.when(j == 0)
    def _():
        h_ref[...] = _rms(x_ref[...], gpre_ref[...]).astype(BF16)
        acc_ref[...] = jnp.zeros_like(acc_ref)

    u = jnp.maximum(_mm(h_ref[...], wu_ref[...]), 0.0)
    acc_ref[...] += _mm((u * u).astype(BF16), wd_ref[...])

    @pl.when(j == pl.num_programs(1) - 1)
    def _():
        o_ref[...] = x_ref[...] + _rms(acc_ref[...], gpost_ref[...])


def _mlp(x, gpre, wu, wd, gpost, l, tm, tf):
    n, d = x.shape
    f = wu.shape[-1]
    return pl.pallas_call(
        _mlp_kernel,
        grid=(n // tm, f // tf),
        in_specs=[pl.BlockSpec((tm, d), lambda i, j: (i, 0)),
                  pl.BlockSpec((None, 1, d), lambda i, j: (l, 0, 0)),
                  pl.BlockSpec((None, d, tf), lambda i, j: (l, 0, j)),
                  pl.BlockSpec((None, tf, d), lambda i, j: (l, j, 0)),
                  pl.BlockSpec((None, 1, d), lambda i, j: (l, 0, 0))],
        out_specs=pl.BlockSpec((tm, d), lambda i, j: (i, 0)),
        out_shape=jax.ShapeDtypeStruct((n, d), F32),
        scratch_shapes=[pltpu.VMEM((tm, d), BF16), pltpu.VMEM((tm, d), F32)],
        compiler_params=_cp("parallel", "arbitrary"),
        name="mlp",
    )(x, gpre, wu, wd, gpost)


def _row_tile(n, target):
    t = min(n, target)
    while n % t:
        t //= 2
    return t


def kernel(x_prompt, x_sample, cache_k, cache_v, cache_logf, page_table, state_mlstm_C, state_mlstm_n, state_mlstm_m, state_ssd, state_conv, w_in, m_b_i, m_b_f, m_norm, f_b_f, s_conv_w, s_conv_b, s_dt_bias, s_A_log, s_D, s_norm, w_branch, w_out, ln_mix_pre, ln_mix_post, ln_mlp_pre, ln_mlp_post, w_up, w_down):
    bp, tp, d = x_prompt.shape
    bs, ts, _ = x_sample.shape
    depth = w_in.shape[0]
    mix = m_norm.shape[-1]
    mh, fh, sh = m_b_i.shape[-1], f_b_f.shape[-1], s_A_log.shape[-1]
    scd = s_conv_w.shape[-1]
    kconv = s_conv_w.shape[1]
    s_hd, ns = state_ssd.shape[-2], state_ssd.shape[-1]
    lay = _Layout(d, mix, mh, fh, sh, scd)
    assert w_in.shape[-1] == lay.in_width
    assert tp % CHUNK == 0 and ts <= SUBLANES
    n_pool, page = cache_k.shape[1], cache_k.shape[2]
    dk, dv, fhd = lay.dk, lay.dv, lay.fhd
    tsp = SUBLANES

    w_in_p = lay.pack(w_in).astype(BF16)
    wb = w_branch.astype(BF16)
    wo = w_out.astype(BF16)
    wu = w_up.astype(BF16)
    wd = w_down.astype(BF16)
    assert lay.small_order == ["ff", "mi", "mf", "sdt"]
    bias_small = jnp.concatenate([f_b_f, m_b_i, m_b_f, s_dt_bias,
                                  jnp.zeros((depth, LANES - lay.small_used), F32)], axis=-1)
    bias_row = bias_small[:, None, :]
    bias_col = bias_small[:, :, None]
    alog = jnp.zeros((depth, LANES), F32).at[:, lay.lane["sdt"]:lay.lane["sdt"] + sh].set(s_A_log)
    alog_row = alog[:, None, :]
    alog_col = alog[:, :, None]
    d_exp = jnp.repeat(s_D, s_hd, axis=-1)[:, None, :]
    r3 = lambda a: a[:, None, :]
    m_norm3, s_norm3, conv_b3 = r3(m_norm), r3(s_norm), r3(s_conv_b)
    g_mix_pre, g_mix_post, g_mlp_pre, g_mlp_post = r3(ln_mix_pre), r3(ln_mix_post), r3(ln_mlp_pre), r3(ln_mlp_post)

    lf_flat = cache_logf.reshape(depth * n_pool, page * fh)
    rsuf, rtot = _lfsuffix(lf_flat, fh, _row_tile(depth * n_pool, 512))
    rsuf = rsuf[:, None, :]
    rtot = rtot[:, None, :]

    np_, ns_ = bp * tp, bs * ts
    xp = x_prompt.reshape(np_, d)
    xs = x_sample.reshape(ns_, d)
    tm_p = _row_tile(np_, 1024)
    tm_s = _row_tile(ns_, 1024)
    tn_in = lay.width // 11 if lay.width % (11 * LANES) == 0 else LANES
    tq, tk = _row_tile(tp, 1024), _row_tile(tp, 512)

    zero_c = jnp.zeros((bp, mh, dv, dk), F32)
    zero_n = jnp.zeros((bp, mh, 1, dk), F32)
    zero_m = jnp.zeros((bp, mh, 1, 1), F32)
    zero_h = jnp.zeros((bp, sh * s_hd, ns), F32)
    zero_cv = jnp.zeros((bp, SUBLANES, scd), F32)

    outs_p = [[] for _ in range(8)]
    outs_s = [[] for _ in range(8)]
    for l in range(depth):
        z = _inproj(xp, g_mix_pre, w_in_p, l, tm_p, tn_in)
        z3 = z.reshape(bp, tp, lay.width)
        hm, c_p, n_p, m_p = _mlstm(z3, lay, CHUNK, bias_row, bias_col, m_norm3, l, zero_c, zero_n, zero_m)
        lf_p, crow = _foxcum(z3, lay, CHUNK, bias_row, bias_col, l)
        hf = _flash(z3, crow[:, :, None, :], lay, tq, tk)
        ysd, h_p = _ssd(z3, lay, CHUNK, bias_row, bias_col, alog_row, alog_col, s_conv_w, conv_b3, d_exp, s_norm3,
                        l, zero_h, zero_cv, ns, s_hd)
        mixed = _merge(hm.reshape(np_, mix), hf.reshape(np_, mix), ysd.reshape(np_, mix), wb, z, lay, l, tm_p, 512)
        xp = _outproj(mixed, wo, g_mix_post, xp, l, _row_tile(np_, 512))
        xp = _mlp(xp, g_mlp_pre, wu, wd, g_mlp_post, l, _row_tile(np_, 512), 512)
        fk_o, fv_o, u_o = lay.dst["fk"], lay.dst["fv"], lay.dst["sxbc"]
        for lst, a in zip(outs_p, (z3[:, :, fk_o:fk_o + mix].reshape(bp, tp, fh, fhd),
                                   z3[:, :, fv_o:fv_o + mix].reshape(bp, tp, fh, fhd),
                                   lf_p, c_p, n_p.reshape(bp, mh, dk), m_p.reshape(bp, mh),
                                   h_p.reshape(bp, sh, s_hd, ns), z3[:, tp - (kconv - 1):, u_o:u_o + scd])):
            lst.append(a)

        zs = _inproj(xs, g_mix_pre, w_in_p, l, tm_s, tn_in)
        zs3 = zs.reshape(bs, ts, lay.width)
        zs8 = jnp.pad(zs3, ((0, 0), (0, tsp - ts), (0, 0)))
        hm_s, c_s, n_s, m_s = _mlstm(zs8, lay, ts, bias_row, bias_col, m_norm3, l, state_mlstm_C[l],
                                     state_mlstm_n[l][:, :, None, :], state_mlstm_m[l][:, :, None, None])
        lf_s, crow_s = _foxcum(zs8, lay, ts, bias_row, bias_col, l)
        fnew = jnp.transpose(crow_s[:, :, :ts], (0, 2, 1)).reshape(bs, 1, ts * fh)
        fnew = jnp.pad(fnew, ((0, 0), (0, 0), (0, LANES - ts * fh)))
        z5 = zs8.reshape(bs, tsp, lay.width // LANES, LANES)
        hf_s = _decode(z5, fnew, cache_k, cache_v, rsuf, rtot, page_table, lay, l, ts)
        cv0 = jnp.pad(state_conv[l], ((0, 0), (SUBLANES - (kconv - 1), 0), (0, 0)))
        ys_s, h_s = _ssd(zs8, lay, ts, bias_row, bias_col, alog_row, alog_col, s_conv_w, conv_b3, d_exp, s_norm3,
                         l, state_ssd[l].reshape(bs, sh * s_hd, ns), cv0, ns, s_hd)
        mixed_s = _merge(hm_s[:, :ts].reshape(ns_, mix), hf_s.reshape(ns_, mix).astype(BF16),
                         ys_s[:, :ts].reshape(ns_, mix), wb, zs, lay, l, tm_s, 512)
        xs = _outproj(mixed_s, wo, g_mix_post, xs, l, tm_s)
        xs = _mlp(xs, g_mlp_pre, wu, wd, g_mlp_post, l, tm_s, 512)
        conv_full = jnp.concatenate([state_conv[l], zs3[:, :, u_o:u_o + scd]], axis=1)
        for lst, a in zip(outs_s, (zs3[:, :, fk_o:fk_o + mix].reshape(bs, ts, fh, fhd),
                                   zs3[:, :, fv_o:fv_o + mix].reshape(bs, ts, fh, fhd),
                                   lf_s[:, :ts], c_s, n_s.reshape(bs, mh, dk), m_s.reshape(bs, mh),
                                   h_s.reshape(bs, sh, s_hd, ns), conv_full[:, -(kconv - 1):])):
            lst.append(a)

    k_p, v_p, lf_pp, c_pp, n_pp, m_pp, ssd_p, conv_p = [jnp.stack(a) for a in outs_p]
    k_s, v_s, lf_ss, c_ss, n_ss, m_ss, ssd_s, conv_s = [jnp.stack(a) for a in outs_s]
    return (xp.reshape(bp, tp, d), xs.reshape(bs, ts, d), k_p, v_p, lf_pp, k_s, v_s, lf_ss,
            c_pp, n_pp, m_pp, c_ss, n_ss, m_ss, ssd_p, ssd_s, conv_p, conv_s)
```

```python
import functools

import jax
import jax.numpy as jnp
from jax import lax
from jax.experimental import pallas as pl
from jax.experimental.pallas import tpu as pltpu

F32 = jnp.float32
BF16 = jnp.bfloat16
HI = lax.Precision.HIGHEST
EPS = 1e-6
NEG_INF = float("-inf")
LANES = 128
SUBLANES = 8
CHUNK = 128
VMEM_LIMIT = 56 * 1024 * 1024


def _cp(*sem, vmem=VMEM_LIMIT):
    return pltpu.CompilerParams(dimension_semantics=sem, vmem_limit_bytes=vmem)


def _nt(a, b):
    return lax.dot_general(a, b, (((1,), (1,)), ((), ())), preferred_element_type=F32)


def _mm(a, b):
    return jnp.dot(a, b, preferred_element_type=F32)


def _mm_exact(a, b):
    return jnp.dot(a, b, preferred_element_type=F32, precision=HI)


def _softplus(x):
    return jnp.maximum(x, 0.0) + jnp.log1p(jnp.exp(-jnp.abs(x)))


def _log_sigmoid(x):
    return -_softplus(-x)


def _sigmoid(x):
    return 1.0 / (1.0 + jnp.exp(-x))


def _silu(x):
    return x * _sigmoid(x)


def _rms(x, g):
    return x * lax.rsqrt(jnp.mean(x * x, axis=-1, keepdims=True) + EPS) * g


def _pad_rows(a, rows):
    if a.shape[0] == rows:
        return a
    return jnp.concatenate([a, jnp.zeros((rows - a.shape[0], a.shape[1]), a.dtype)], axis=0)


def _imod(x, n):
    assert n & (n - 1) == 0
    return x & (n - 1)


def _idiv(x, n):
    assert n & (n - 1) == 0
    return x >> (n.bit_length() - 1)


def _pick_col(a, sel):
    return jnp.sum(jnp.where(sel, a, 0.0), axis=1, keepdims=True)


def _pick_row(a, sel):
    return jnp.sum(jnp.where(sel, a, 0.0), axis=0, keepdims=True)


class _Layout:
    def __init__(self, d_model, mix, mh, fh, sh, scd):
        self.d, self.mix, self.mh, self.fh, self.sh, self.scd = d_model, mix, mh, fh, sh, scd
        self.dk = mix // mh // 2
        self.dv = mix // mh
        self.fhd = mix // fh
        src = [("mq", mh * self.dk), ("mk", mh * self.dk), ("mv", mix), ("mi", mh), ("mf", mh), ("mo", mix),
               ("fq", mix), ("fk", mix), ("fv", mix), ("ff", fh), ("sz", mix), ("sxbc", scd), ("sdt", sh),
               ("g", 3 * d_model)]
        self.src = {}
        off = 0
        for name, w in src:
            self.src[name] = (off, w)
            off += w
        self.in_width = off
        order = ["mq", "mk", "mv", "sxbc", "mo", "fq", "fk", "fv", "sz", "g"]
        self.dst = {}
        off = 0
        for name in order:
            self.dst[name] = off
            off += self.src[name][1]
        self.small = off
        self.small_order = ["ff", "mi", "mf", "sdt"]
        self.lane = {}
        l = 0
        for name in self.small_order:
            self.lane[name] = l
            l += self.src[name][1]
        assert l <= LANES
        self.small_used = l
        self.order = order
        self.width = off + LANES

    def pack(self, a):
        parts = [a[..., self.src[n][0]:self.src[n][0] + self.src[n][1]] for n in self.order]
        parts += [a[..., self.src[n][0]:self.src[n][0] + self.src[n][1]] for n in self.small_order]
        pad = LANES - self.small_used
        parts.append(jnp.zeros(a.shape[:-1] + (pad,), a.dtype))
        return jnp.concatenate(parts, axis=-1)


def _inproj_kernel(x_ref, g_ref, w_ref, o_ref, h_ref):
    @pl.when(pl.program_id(1) == 0)
    def _():
        h_ref[...] = _rms(x_ref[...], g_ref[...]).astype(BF16)

    o_ref[...] = _mm(h_ref[...], w_ref[...])


def _inproj(x, g, w, l, tm, tn):
    n, d = x.shape
    width = w.shape[-1]
    return pl.pallas_call(
        _inproj_kernel,
        grid=(n // tm, width // tn),
        in_specs=[pl.BlockSpec((tm, d), lambda i, j: (i, 0)),
                  pl.BlockSpec((None, 1, d), lambda i, j: (l, 0, 0)),
                  pl.BlockSpec((None, d, tn), lambda i, j: (l, 0, j))],
        out_specs=pl.BlockSpec((tm, tn), lambda i, j: (i, j)),
        out_shape=jax.ShapeDtypeStruct((n, width), F32),
        scratch_shapes=[pltpu.VMEM((tm, d), BF16)],
        compiler_params=_cp("parallel", "arbitrary"),
        name="inproj",
    )(x, g, w)


def _mlstm_kernel(q_ref, k_ref, v_ref, og_ref, s_ref, brow_ref, bcol_ref, mn_ref, c0_ref, n0_ref, m0_ref,
                  hm_ref, cout_ref, nout_ref, mout_ref, ct_ref, n_ref, m_ref, *, L, lr, lane_i, lane_f, kscale):
    h = pl.program_id(1)
    c = pl.program_id(2)
    nc = pl.num_programs(2)

    @pl.when(c == 0)
    def _():
        ct_ref[...] = c0_ref[0, 0].T
        n_ref[...] = n0_ref[0, 0]
        m_ref[...] = m0_ref[0, 0]

    q = _pad_rows(q_ref[0], L)
    k = _pad_rows(k_ref[0], L) * kscale
    v = _pad_rows(v_ref[0], L)
    gates = _pad_rows(s_ref[0], L)
    row = lax.broadcasted_iota(jnp.int32, (L, 1), 0)
    col = lax.broadcasted_iota(jnp.int32, (1, L), 1)
    lane = lax.broadcasted_iota(jnp.int32, (1, LANES), 1)
    sub = lax.broadcasted_iota(jnp.int32, (LANES, 1), 0)
    valid_c = row < lr
    valid_r = col < lr
    causal = col <= row
    gb = gates + brow_ref[...]
    gt = gates.T + bcol_ref[...]
    lf = jnp.where(valid_c, _log_sigmoid(gb), 0.0)
    lft = jnp.where(valid_r, _log_sigmoid(gt), 0.0)
    bc_all = _mm_exact(causal.astype(F32), lf)
    br_all = _mm_exact(lft, (row <= col).astype(F32))
    i_c = jnp.where(valid_c, _pick_col(gb, lane == lane_i + h), NEG_INF)
    i_r = jnp.where(valid_r, _pick_row(gt, sub == lane_i + h), NEG_INF)
    b_c = _pick_col(bc_all, lane == lane_f + h)
    b_r = _pick_row(br_all, sub == lane_f + h)

    m_prev = m_ref[...]
    log_d = jnp.where(causal, b_c + (i_r - b_r), NEG_INF)
    inter = b_c + m_prev
    m_t = jnp.maximum(inter, jnp.max(log_d, axis=1, keepdims=True))
    w_inter = jnp.exp(inter - m_t)
    qb = q.astype(BF16)
    kb = k.astype(BF16)
    vb = v.astype(BF16)
    s = _nt(qb, kb) * jnp.exp(log_d - m_t)
    ct = ct_ref[...]
    nvec = n_ref[...]
    num = _mm(s.astype(BF16), vb) + w_inter * _mm(qb, ct.astype(BF16))
    nq = jnp.sum(qb.astype(F32) * nvec.astype(BF16).astype(F32), axis=1, keepdims=True)
    den = jnp.sum(s, axis=1, keepdims=True) + w_inter * nq
    hh = num / jnp.maximum(jnp.abs(den), jnp.exp(-m_t))

    b_last = b_c[L - 1:L, :]
    le_c = b_last - b_c + i_c
    le_r = b_last - b_r + i_r
    m_new = jnp.maximum(b_last + m_prev, jnp.max(le_r, axis=1, keepdims=True))
    we_c = jnp.exp(le_c - m_new)
    we_r = jnp.exp(le_r - m_new)
    ws = jnp.exp(b_last + m_prev - m_new)
    ct_ref[...] = ws * ct + _mm(k.T.astype(BF16), (we_c * v).astype(BF16))
    n_ref[...] = ws * nvec + _mm(jnp.broadcast_to(we_r, (SUBLANES, L)).astype(BF16), kb)[0:1]
    m_ref[...] = m_new

    y = _rms(hh, mn_ref[...]) * _sigmoid(_pad_rows(og_ref[0], L))
    hm_ref[0] = y[:hm_ref.shape[1]].astype(BF16)

    @pl.when(c == nc - 1)
    def _():
        cout_ref[0, 0] = ct_ref[...].T
        nout_ref[0, 0] = n_ref[...]
        mout_ref[0, 0] = m_ref[...]


def _mlstm(z3, lay, lr, bias_row, bias_col, m_norm, l, c0, n0, m0, ls):
    bsz, t, _ = z3.shape
    L = CHUNK
    rb = min(t, L)
    nc = t // rb
    mh, dk, dv = lay.mh, lay.dk, lay.dv
    kern = functools.partial(_mlstm_kernel, L=L, lr=lr, lane_i=lay.lane["mi"], lane_f=lay.lane["mf"],
                             kscale=float(dk) ** -0.5)
    qo, ko, vo, oo, so = (lay.dst["mq"] // dk, lay.dst["mk"] // dk, lay.dst["mv"] // dv, lay.dst["mo"] // dv,
                          lay.small // LANES)
    return pl.pallas_call(
        kern,
        grid=(bsz, mh, nc),
        in_specs=[pl.BlockSpec((1, rb, dk), lambda b, h, c: (b, c, qo + h)),
                  pl.BlockSpec((1, rb, dk), lambda b, h, c: (b, c, ko + h)),
                  pl.BlockSpec((1, rb, dv), lambda b, h, c: (b, c, vo + h)),
                  pl.BlockSpec((1, rb, dv), lambda b, h, c: (b, c, oo + h)),
                  pl.BlockSpec((1, rb, LANES), lambda b, h, c: (b, c, so)),
                  pl.BlockSpec((None, 1, LANES), lambda b, h, c: (l, 0, 0)),
                  pl.BlockSpec((None, LANES, 1), lambda b, h, c: (l, 0, 0)),
                  pl.BlockSpec((None, 1, dv), lambda b, h, c: (l, 0, h)),
                  pl.BlockSpec((None, 1, 1, dv, dk), lambda b, h, c: (ls, b, h, 0, 0)),
                  pl.BlockSpec((None, 1, 1, 1, dk), lambda b, h, c: (ls, b, h, 0, 0)),
                  pl.BlockSpec((None, 1, 1, 1, 1), lambda b, h, c: (ls, b, h, 0, 0))],
        out_specs=[pl.BlockSpec((1, rb, dv), lambda b, h, c: (b, c, h)),
                   pl.BlockSpec((1, 1, dv, dk), lambda b, h, c: (b, h, 0, 0)),
                   pl.BlockSpec((1, 1, 1, dk), lambda b, h, c: (b, h, 0, 0)),
                   pl.BlockSpec((1, 1, 1, 1), lambda b, h, c: (b, h, 0, 0))],
        out_shape=[jax.ShapeDtypeStruct((bsz, t, mh * dv), BF16),
                   jax.ShapeDtypeStruct((bsz, mh, dv, dk), F32),
                   jax.ShapeDtypeStruct((bsz, mh, 1, dk), F32),
                   jax.ShapeDtypeStruct((bsz, mh, 1, 1), F32)],
        scratch_shapes=[pltpu.VMEM((dk, dv), F32), pltpu.VMEM((1, dk), F32), pltpu.VMEM((1, 1), F32)],
        compiler_params=_cp("parallel", "parallel", "arbitrary"),
        name="mlstm",
    )(z3, z3, z3, z3, z3, bias_row, bias_col, m_norm, c0, n0, m0)


def _foxcum_kernel(s_ref, brow_ref, bcol_ref, lf_ref, crow_ref, carry_ref, *, L, lr, nh):
    c = pl.program_id(1)

    @pl.when(c == 0)
    def _():
        carry_ref[...] = jnp.zeros_like(carry_ref)

    gates = _pad_rows(s_ref[0], L)
    row = lax.broadcasted_iota(jnp.int32, (L, 1), 0)
    col = lax.broadcasted_iota(jnp.int32, (1, L), 1)
    lf = _log_sigmoid(gates + brow_ref[...])
    lf_ref[0] = lf[:lf_ref.shape[1], 0:nh]
    lft = jnp.where(col < lr, _log_sigmoid(gates.T + bcol_ref[...]), 0.0)
    csum = _mm_exact(lft[0:nh], (row <= col).astype(F32)) + carry_ref[...]
    crow_ref[0] = csum
    carry_ref[...] = csum[:, L - 1:L]


def _foxcum(z3, lay, lr, bias_row, bias_col, l):
    bsz, t, _ = z3.shape
    L = CHUNK
    rb = min(t, L)
    nc = t // rb
    fh = lay.fh
    so = lay.small // LANES
    return pl.pallas_call(
        functools.partial(_foxcum_kernel, L=L, lr=lr, nh=fh),
        grid=(bsz, nc),
        in_specs=[pl.BlockSpec((1, rb, LANES), lambda b, c: (b, c, so)),
                  pl.BlockSpec((None, 1, LANES), lambda b, c: (l, 0, 0)),
                  pl.BlockSpec((None, LANES, 1), lambda b, c: (l, 0, 0))],
        out_specs=[pl.BlockSpec((1, rb, fh), lambda b, c: (b, c, 0)),
                   pl.BlockSpec((1, fh, L), lambda b, c: (b, 0, c))],
        out_shape=[jax.ShapeDtypeStruct((bsz, t, fh), F32),
                   jax.ShapeDtypeStruct((bsz, fh, nc * L), F32)],
        scratch_shapes=[pltpu.VMEM((fh, 1), F32)],
        compiler_params=_cp("parallel", "arbitrary"),
        name="foxcum",
    )(z3, bias_row, bias_col)


def _flash_kernel(q_ref, k_ref, v_ref, c_ref, o_ref, kb_ref, vb_ref, *, tq, scale):
    qi = pl.program_id(2)

    @pl.when(qi == 0)
    def _():
        kb_ref[...] = k_ref[0].astype(BF16)
        vb_ref[...] = v_ref[0].astype(BF16)

    q = q_ref[0].astype(BF16)
    hd = q.shape[1]
    causal = lax.broadcasted_iota(jnp.int32, (1, tq), 1) <= lax.broadcasted_iota(jnp.int32, (tq, 1), 0)

    def block(j, carry, diagonal):
        m_prev, l_prev, acc = carry
        off = pl.multiple_of(j * tq, tq)
        s = _nt(q, kb_ref[pl.ds(off, tq), :]) * scale - c_ref[0, 0, j]
        if diagonal:
            s = jnp.where(causal, s, NEG_INF)
        m_new = jnp.maximum(m_prev, jnp.max(s, axis=1, keepdims=True))
        alpha = jnp.exp(m_prev - m_new)
        p = jnp.exp(s - m_new)
        l_new = alpha * l_prev + jnp.sum(p, axis=1, keepdims=True)
        acc = alpha * acc + _mm(p.astype(BF16), vb_ref[pl.ds(off, tq), :])
        return m_new, l_new, acc

    init = (jnp.full((tq, 1), NEG_INF, F32), jnp.zeros((tq, 1), F32), jnp.zeros((tq, hd), F32))
    carry = lax.fori_loop(0, qi, lambda j, c: block(j, c, False), init)
    _, l_fin, acc = block(qi, carry, True)
    o_ref[0] = (acc / l_fin).astype(BF16)


def _flash(z3, crow, lay, tq):
    bsz, t, _ = z3.shape
    fh, hd = lay.fh, lay.fhd
    qo, ko, vo = lay.dst["fq"] // hd, lay.dst["fk"] // hd, lay.dst["fv"] // hd
    nq = t // tq
    c5 = crow.reshape(bsz, fh, nq, 1, tq)
    return pl.pallas_call(
        functools.partial(_flash_kernel, tq=tq, scale=float(hd) ** -0.5),
        grid=(bsz, fh, nq),
        in_specs=[pl.BlockSpec((1, tq, hd), lambda b, h, qi: (b, qi, qo + h)),
                  pl.BlockSpec((1, t, hd), lambda b, h, qi: (b, 0, ko + h)),
                  pl.BlockSpec((1, t, hd), lambda b, h, qi: (b, 0, vo + h)),
                  pl.BlockSpec((1, 1, nq, 1, tq), lambda b, h, qi: (b, h, 0, 0, 0))],
        out_specs=pl.BlockSpec((1, tq, hd), lambda b, h, qi: (b, qi, h)),
        out_shape=jax.ShapeDtypeStruct((bsz, t, fh * hd), BF16),
        scratch_shapes=[pltpu.VMEM((t, hd), BF16), pltpu.VMEM((t, hd), BF16)],
        compiler_params=_cp("parallel", "parallel", "arbitrary"),
        name="fox_flash",
    )(z3, z3, z3, c5)


def _lfsuffix_kernel(x_ref, o_ref, mat_ref, *, nh):
    w = x_ref.shape[1]

    @pl.when(pl.program_id(0) == 0)
    def _():
        ri = lax.broadcasted_iota(jnp.int32, (w, 1), 0)
        ci = lax.broadcasted_iota(jnp.int32, (1, w), 1)
        same = _imod(ri, nh) == _imod(ci, nh)
        mat_ref[:, 0:w] = jnp.where(same & (_idiv(ri, nh) > _idiv(ci, nh)), 1.0, 0.0)
        mat_ref[:, w:2 * w] = jnp.where(same, 1.0, 0.0)

    o_ref[...] = _mm_exact(x_ref[...], mat_ref[...])


def _lfsuffix(lf_flat, nh, tm):
    n, w = lf_flat.shape
    return pl.pallas_call(
        functools.partial(_lfsuffix_kernel, nh=nh),
        grid=(n // tm,),
        in_specs=[pl.BlockSpec((tm, w), lambda i: (i, 0))],
        out_specs=pl.BlockSpec((tm, 2 * w), lambda i: (i, 0)),
        out_shape=jax.ShapeDtypeStruct((n, 2 * w), F32),
        scratch_shapes=[pltpu.VMEM((w, 2 * w), F32)],
        compiler_params=_cp("arbitrary"),
        name="lf_suffix",
    )(lf_flat)


def _decode_kernel(pt_ref, q_ref, kn_ref, vn_ref, fn_ref, *refs, scale, nq, nh, npg):
    k_refs, v_refs, rt_refs = refs[0:npg], refs[npg:2 * npg], refs[2 * npg:3 * npg]
    o_ref, m_ref, l_ref, acc_ref, tail_ref = refs[3 * npg:]
    j = pl.program_id(1)
    nj = pl.num_programs(1)
    rows = nq * nh
    hd = q_ref.shape[-1]
    w = k_refs[0].shape[0] * k_refs[0].shape[1]

    @pl.when(j == 0)
    def _():
        m_ref[...] = jnp.full_like(m_ref, NEG_INF)
        l_ref[...] = jnp.zeros_like(l_ref)
        acc_ref[...] = jnp.zeros_like(acc_ref)
        tail_ref[...] = jnp.zeros_like(tail_ref)

    qs = q_ref[0].reshape(rows, hd).astype(BF16)
    row_h = _imod(lax.broadcasted_iota(jnp.int32, (rows, 1), 0), nh)

    def update(scores, values):
        m_prev = m_ref[...]
        smax = scores[0]
        for s in scores[1:]:
            smax = jnp.maximum(smax, s)
        m_new = jnp.maximum(m_prev, jnp.max(smax, axis=1, keepdims=True))
        alpha = jnp.exp(m_prev - m_new)
        psum = None
        pv = None
        for s, vb in zip(scores, values):
            p = jnp.exp(s - m_new)
            psum = p if psum is None else psum + p
            d = _mm(p.astype(BF16), vb)
            pv = d if pv is None else pv + d
        l_ref[...] = alpha * l_ref[...] + jnp.sum(psum, axis=1, keepdims=True)
        acc_ref[...] = alpha * acc_ref[...] + pv
        m_ref[...] = m_new

    same_head = _imod(lax.broadcasted_iota(jnp.int32, (1, w), 1), nh) == row_h
    tail = tail_ref[...]
    scores, values = [], []
    for i in range(npg):
        kf = k_refs[i][...].reshape(w, hd).astype(BF16)
        rt = rt_refs[i][0]
        s = _nt(qs, kf) * scale + (rt[:, 0:w] + tail)
        scores.append(jnp.where(same_head, s, NEG_INF))
        values.append(v_refs[i][...].reshape(w, hd).astype(BF16))
        tail = tail + rt[:, w:2 * w]
    tail_ref[...] = tail
    update(scores, values)

    @pl.when(j == nj - 1)
    def _():
        kn = _pad_rows(kn_ref[0].reshape(rows, hd), LANES).astype(BF16)
        vn = _pad_rows(vn_ref[0].reshape(rows, hd), LANES).astype(BF16)
        lane = lax.broadcasted_iota(jnp.int32, (1, LANES), 1)
        row_q = _idiv(lax.broadcasted_iota(jnp.int32, (rows, 1), 0), nh)
        ok = (_imod(lane, nh) == row_h) & (_idiv(lane, nh) <= row_q) & (lane < rows)
        s_new = _nt(qs, kn) * scale - fn_ref[0]
        update([jnp.where(ok, s_new, NEG_INF)], [vn])
        o_ref[0] = (acc_ref[...] / l_ref[...]).reshape(nq, nh, hd)


def _decode(z5, fnew, cache_k, cache_v, rt, page_table, lay, l, nq, npg):
    bsz = z5.shape[0]
    fh, hd = lay.fh, lay.fhd
    n_pool, page = cache_k.shape[1], cache_k.shape[2]
    n_pages = page_table.shape[1]
    w = page * fh
    qo, ko, vo = lay.dst["fq"] // (fh * hd), lay.dst["fk"] // (fh * hd), lay.dst["fv"] // (fh * hd)

    def kv_spec(i):
        return pl.BlockSpec((None, None, page, fh, hd),
                            lambda b, j, pt: (l, pt[b, n_pages - 1 - (j * npg + i)], 0, 0, 0))

    def rt_spec(i):
        return pl.BlockSpec((1, 1, 2 * w), lambda b, j, pt: (l * n_pool + pt[b, n_pages - 1 - (j * npg + i)], 0, 0))

    grid_spec = pltpu.PrefetchScalarGridSpec(
        num_scalar_prefetch=1,
        grid=(bsz, n_pages // npg),
        in_specs=[pl.BlockSpec((1, nq, fh, hd), lambda b, j, pt: (b, 0, qo, 0)),
                  pl.BlockSpec((1, nq, fh, hd), lambda b, j, pt: (b, 0, ko, 0)),
                  pl.BlockSpec((1, nq, fh, hd), lambda b, j, pt: (b, 0, vo, 0)),
                  pl.BlockSpec((1, 1, LANES), lambda b, j, pt: (b, 0, 0))]
                 + [kv_spec(i) for i in range(npg)] + [kv_spec(i) for i in range(npg)]
                 + [rt_spec(i) for i in range(npg)],
        out_specs=pl.BlockSpec((1, nq, fh, hd), lambda b, j, pt: (b, 0, 0, 0)),
        scratch_shapes=[pltpu.VMEM((nq * fh, 1), F32), pltpu.VMEM((nq * fh, 1), F32),
                        pltpu.VMEM((nq * fh, hd), F32), pltpu.VMEM((1, w), F32)],
    )
    return pl.pallas_call(
        functools.partial(_decode_kernel, scale=float(hd) ** -0.5, nq=nq, nh=fh, npg=npg),
        grid_spec=grid_spec,
        out_shape=jax.ShapeDtypeStruct((bsz, nq, fh, hd), F32),
        compiler_params=_cp("parallel", "arbitrary"),
        name="fox_decode",
    )(page_table, z5, z5, z5, fnew, *([cache_k] * npg), *([cache_v] * npg), *([rt] * npg))


def _ssd_kernel(u_ref, sz_ref, s_ref, brow_ref, bcol_ref, alrow_ref, alcol_ref, cw_ref, cb_ref, dexp_ref, sn_ref,
                h0_ref, cv0_ref, y_ref, hout_ref, xs_ref, ht_ref, *, L, lr, ng, hpg, hd, ns, lane_dt, kconv):
    c = pl.program_id(1)
    nc = pl.num_programs(1)
    mix = ng * hpg * hd
    gw = hpg * hd

    @pl.when(c == 0)
    def _():
        xs_ref[0:SUBLANES, :] = cv0_ref[0]
        for g in range(ng):
            ht_ref[g] = h0_ref[0, g * gw:(g + 1) * gw, :].T

    @pl.when(c > 0)
    def _():
        xs_ref[0:SUBLANES, :] = xs_ref[L:L + SUBLANES, :]

    xs_ref[SUBLANES:SUBLANES + L, :] = _pad_rows(u_ref[0], L)
    conv = cb_ref[...]
    for jj in range(kconv):
        off = SUBLANES - (kconv - 1) + jj
        conv = conv + cw_ref[jj:jj + 1, :] * xs_ref[pl.ds(off, L), :]
    xbc = _silu(conv)
    x = xbc[:, :mix]

    gates = _pad_rows(s_ref[0], L)
    row = lax.broadcasted_iota(jnp.int32, (L, 1), 0)
    col = lax.broadcasted_iota(jnp.int32, (1, L), 1)
    lane = lax.broadcasted_iota(jnp.int32, (1, LANES), 1)
    causal = col <= row
    dt_c = jnp.where(row < lr, _softplus(gates + brow_ref[...]), 0.0)
    dt_r = jnp.where(col < lr, _softplus(gates.T + bcol_ref[...]), 0.0)
    cum_c = _mm_exact(causal.astype(F32), dt_c * (-jnp.exp(alrow_ref[...])))
    cum_r = _mm_exact(dt_r * (-jnp.exp(alcol_ref[...])), (row <= col).astype(F32))
    lane_grp = _idiv(lax.broadcasted_iota(jnp.int32, (1, gw), 1), hd)

    ys = []
    for g in range(ng):
        bg = xbc[:, mix + g * ns:mix + (g + 1) * ns]
        cg = xbc[:, mix + (ng + g) * ns:mix + (ng + g + 1) * ns]
        bb = bg.astype(BF16)
        cb = cg.astype(BF16)
        cbm = _nt(cb, bb)
        xg = x[:, g * gw:(g + 1) * gw]
        ht = ht_ref[g]
        y = jnp.zeros((L, gw), F32)
        ecum = jnp.zeros((L, gw), F32)
        wend = jnp.zeros((L, gw), F32)
        dec = jnp.zeros((1, gw), F32)
        for r in range(hpg):
            idx = lane_dt + g * hpg + r
            cc = _pick_col(cum_c, lane == idx)
            dtc = _pick_col(dt_c, lane == idx)
            cr = cum_r[idx:idx + 1, :]
            dtr = dt_r[idx:idx + 1, :]
            wmat = cbm * jnp.exp(jnp.where(causal, cc - cr, NEG_INF)) * dtr
            sel = lane_grp == r
            y = y + _mm(wmat.astype(BF16), jnp.where(sel, xg, 0.0).astype(BF16))
            ce = cc[L - 1:L, :]
            ecum = jnp.where(sel, jnp.exp(cc), ecum)
            wend = jnp.where(sel, jnp.exp(ce - cc) * dtc, wend)
            dec = jnp.where(sel, jnp.exp(ce), dec)
        y = y + ecum * _mm(cb, ht.astype(BF16))
        ht_ref[g] = dec * ht + _mm(bg.T.astype(BF16), (wend * xg).astype(BF16))
        ys.append(y + dexp_ref[:, g * gw:(g + 1) * gw] * xg)
    yall = jnp.concatenate(ys, axis=1) * _silu(_pad_rows(sz_ref[0], L))
    y_ref[0] = _rms(yall, sn_ref[...])[:y_ref.shape[1]].astype(BF16)

    @pl.when(c == nc - 1)
    def _():
        for g in range(ng):
            hout_ref[0, g * gw:(g + 1) * gw, :] = ht_ref[g].T


def _ssd(z3, lay, lr, bias_row, bias_col, alog_row, alog_col, conv_w, conv_b, d_exp, s_norm, l, h0, ls, cv0, ns, hd):
    bsz, t, _ = z3.shape
    L = CHUNK
    rb = min(t, L)
    nc = t // rb
    mix, scd, sh = lay.mix, lay.scd, lay.sh
    ng = (scd - mix) // (2 * ns)
    hpg = sh // ng
    kconv = conv_w.shape[1]
    kern = functools.partial(_ssd_kernel, L=L, lr=lr, ng=ng, hpg=hpg, hd=hd, ns=ns, lane_dt=lay.lane["sdt"],
                             kconv=kconv)
    uo, zo, so = lay.dst["sxbc"] // scd, lay.dst["sz"] // mix, lay.small // LANES
    return pl.pallas_call(
        kern,
        grid=(bsz, nc),
        in_specs=[pl.BlockSpec((1, rb, scd), lambda b, c: (b, c, uo)),
                  pl.BlockSpec((1, rb, mix), lambda b, c: (b, c, zo)),
                  pl.BlockSpec((1, rb, LANES), lambda b, c: (b, c, so)),
                  pl.BlockSpec((None, 1, LANES), lambda b, c: (l, 0, 0)),
                  pl.BlockSpec((None, LANES, 1), lambda b, c: (l, 0, 0)),
                  pl.BlockSpec((None, 1, LANES), lambda b, c: (l, 0, 0)),
                  pl.BlockSpec((None, LANES, 1), lambda b, c: (l, 0, 0)),
                  pl.BlockSpec((None, kconv, scd), lambda b, c: (l, 0, 0)),
                  pl.BlockSpec((None, 1, scd), lambda b, c: (l, 0, 0)),
                  pl.BlockSpec((None, 1, mix), lambda b, c: (l, 0, 0)),
                  pl.BlockSpec((None, 1, mix), lambda b, c: (l, 0, 0)),
                  pl.BlockSpec((None, 1, sh * hd, ns), lambda b, c: (ls, b, 0, 0)),
                  pl.BlockSpec((1, SUBLANES, scd), lambda b, c: (b, 0, 0))],
        out_specs=[pl.BlockSpec((1, rb, mix), lambda b, c: (b, c, 0)),
                   pl.BlockSpec((1, sh * hd, ns), lambda b, c: (b, 0, 0))],
        out_shape=[jax.ShapeDtypeStruct((bsz, t, mix), BF16),
                   jax.ShapeDtypeStruct((bsz, sh * hd, ns), F32)],
        scratch_shapes=[pltpu.VMEM((L + 2 * SUBLANES, scd), F32), pltpu.VMEM((ng, ns, hpg * hd), F32)],
        compiler_params=_cp("parallel", "arbitrary"),
        name="ssd",
    )(z3, z3, z3, bias_row, bias_col, alog_row, alog_col, conv_w, conv_b, d_exp, s_norm, h0, cv0)


def _merge_kernel(hm_ref, hf_ref, ys_ref, wb_ref, g0_ref, g1_ref, g2_ref, o_ref):
    acc = _sigmoid(g0_ref[...]) * _mm(hm_ref[...], wb_ref[0])
    acc = acc + _sigmoid(g1_ref[...]) * _mm(hf_ref[...], wb_ref[1])
    acc = acc + _sigmoid(g2_ref[...]) * _mm(ys_ref[...], wb_ref[2])
    o_ref[...] = acc.astype(BF16)


def _merge(hm, hf, ys, wb, z, lay, l, tm, tn):
    n, mix = hm.shape
    d = lay.d
    go = lay.dst["g"] // tn
    per = d // tn
    gspec = lambda k: pl.BlockSpec((tm, tn), lambda i, j: (i, go + k * per + j))
    act = pl.BlockSpec((tm, mix), lambda i, j: (i, 0))
    return pl.pallas_call(
        _merge_kernel,
        grid=(n // tm, d // tn),
        in_specs=[act, act, act, pl.BlockSpec((None, 3, mix, tn), lambda i, j: (l, 0, 0, j)),
                  gspec(0), gspec(1), gspec(2)],
        out_specs=pl.BlockSpec((tm, tn), lambda i, j: (i, j)),
        out_shape=jax.ShapeDtypeStruct((n, d), BF16),
        compiler_params=_cp("parallel", "arbitrary"),
        name="merge",
    )(hm, hf, ys, wb, z, z, z)


def _outproj_kernel(a_ref, w_ref, g_ref, x_ref, o_ref):
    o_ref[...] = x_ref[...] + _rms(_mm(a_ref[...], w_ref[...]), g_ref[...])


def _outproj(a, w, g, x, l, tm):
    n, d = x.shape
    return pl.pallas_call(
        _outproj_kernel,
        grid=(n // tm,),
        in_specs=[pl.BlockSpec((tm, d), lambda i: (i, 0)),
                  pl.BlockSpec((None, d, d), lambda i: (l, 0, 0)),
                  pl.BlockSpec((None, 1, d), lambda i: (l, 0, 0)),
                  pl.BlockSpec((tm, d), lambda i: (i, 0))],
        out_specs=pl.BlockSpec((tm, d), lambda i: (i, 0)),
        out_shape=jax.ShapeDtypeStruct((n, d), F32),
        compiler_params=_cp("parallel"),
        name="outproj",
    )(a, w, g, x)


def _mlp_kernel(x_ref, gpre_ref, wu_ref, wd_ref, gpost_ref, o_ref, h_ref, acc_ref):
    j = pl.program_id(1)

    @pl.when(j == 0)
    def _():
        h_ref[...] = _rms(x_ref[...], gpre_ref[...]).astype(BF16)
        acc_ref[...] = jnp.zeros_like(acc_ref)

    u = jnp.maximum(_mm(h_ref[...], wu_ref[...]), 0.0)
    acc_ref[...] += _mm((u * u).astype(BF16), wd_ref[...])

    @pl.when(j == pl.num_programs(1) - 1)
    def _():
        o_ref[...] = x_ref[...] + _rms(acc_ref[...], gpost_ref[...])


def _mlp(x, gpre, wu, wd, gpost, l, tm, tf):
    n, d = x.shape
    f = wu.shape[-1]
    return pl.pallas_call(
        _mlp_kernel,
        grid=(n // tm, f // tf),
        in_specs=[pl.BlockSpec((tm, d), lambda i, j: (i, 0)),
                  pl.BlockSpec((None, 1, d), lambda i, j: (l, 0, 0)),
                  pl.BlockSpec((None, d, tf), lambda i, j: (l, 0, j)),
                  pl.BlockSpec((None, tf, d), lambda i, j: (l, j, 0)),
                  pl.BlockSpec((None, 1, d), lambda i, j: (l, 0, 0))],
        out_specs=pl.BlockSpec((tm, d), lambda i, j: (i, 0)),
        out_shape=jax.ShapeDtypeStruct((n, d), F32),
        scratch_shapes=[pltpu.VMEM((tm, d), BF16), pltpu.VMEM((tm, d), F32)],
        compiler_params=_cp("parallel", "arbitrary"),
        name="mlp",
    )(x, gpre, wu, wd, gpost)


def _row_tile(n, target):
    t = min(n, target)
    while n % t:
        t //= 2
    return t


def kernel(x_prompt, x_sample, cache_k, cache_v, cache_logf, page_table, state_mlstm_C, state_mlstm_n, state_mlstm_m, state_ssd, state_conv, w_in, m_b_i, m_b_f, m_norm, f_b_f, s_conv_w, s_conv_b, s_dt_bias, s_A_log, s_D, s_norm, w_branch, w_out, ln_mix_pre, ln_mix_post, ln_mlp_pre, ln_mlp_post, w_up, w_down):
    bp, tp, d = x_prompt.shape
    bs, ts, _ = x_sample.shape
    depth = w_in.shape[0]
    mix = m_norm.shape[-1]
    mh, fh, sh = m_b_i.shape[-1], f_b_f.shape[-1], s_A_log.shape[-1]
    scd = s_conv_w.shape[-1]
    kconv = s_conv_w.shape[1]
    s_hd, ns = state_ssd.shape[-2], state_ssd.shape[-1]
    lay = _Layout(d, mix, mh, fh, sh, scd)
    assert w_in.shape[-1] == lay.in_width
    assert tp % CHUNK == 0 and ts <= SUBLANES
    n_pool, page = cache_k.shape[1], cache_k.shape[2]
    dk, dv, fhd = lay.dk, lay.dv, lay.fhd
    tsp = SUBLANES

    w_in_p = lay.pack(w_in).astype(BF16)
    wb = w_branch.astype(BF16)
    wo = w_out.astype(BF16)
    wu = w_up.astype(BF16)
    wd = w_down.astype(BF16)
    assert lay.small_order == ["ff", "mi", "mf", "sdt"]
    bias_small = jnp.concatenate([f_b_f, m_b_i, m_b_f, s_dt_bias,
                                  jnp.zeros((depth, LANES - lay.small_used), F32)], axis=-1)
    bias_row = bias_small[:, None, :]
    bias_col = bias_small[:, :, None]
    alog = jnp.zeros((depth, LANES), F32).at[:, lay.lane["sdt"]:lay.lane["sdt"] + sh].set(s_A_log)
    alog_row = alog[:, None, :]
    alog_col = alog[:, :, None]
    d_exp = jnp.repeat(s_D, s_hd, axis=-1)[:, None, :]
    r3 = lambda a: a[:, None, :]
    m_norm3, s_norm3, conv_b3 = r3(m_norm), r3(s_norm), r3(s_conv_b)
    g_mix_pre, g_mix_post, g_mlp_pre, g_mlp_post = r3(ln_mix_pre), r3(ln_mix_post), r3(ln_mlp_pre), r3(ln_mlp_post)

    lf_flat = cache_logf.reshape(depth * n_pool, page * fh)
    rt = _lfsuffix(lf_flat, fh, _row_tile(depth * n_pool, 512))[:, None, :]
    npg = _row_tile(page_table.shape[1], 16)

    np_, ns_ = bp * tp, bs * ts
    xp = x_prompt.reshape(np_, d)
    xs = x_sample.reshape(ns_, d)
    tm_p = _row_tile(np_, 1024)
    tm_s = _row_tile(ns_, 1024)
    tn_in = lay.width // 11 if lay.width % (11 * LANES) == 0 else LANES
    tq = _row_tile(tp, 512)

    zero_c = jnp.zeros((1, bp, mh, dv, dk), F32)
    zero_n = jnp.zeros((1, bp, mh, 1, dk), F32)
    zero_m = jnp.zeros((1, bp, mh, 1, 1), F32)
    zero_h = jnp.zeros((1, bp, sh * s_hd, ns), F32)
    c0_s = state_mlstm_C
    n0_s = state_mlstm_n[:, :, :, None, :]
    m0_s = state_mlstm_m[:, :, :, None, None]
    h0_s = state_ssd.reshape(depth, bs, sh * s_hd, ns)
    zero_cv = jnp.zeros((bp, SUBLANES, scd), F32)

    outs_p = [[] for _ in range(8)]
    outs_s = [[] for _ in range(8)]
    for l in range(depth):
        z = _inproj(xp, g_mix_pre, w_in_p, l, tm_p, tn_in)
        z3 = z.reshape(bp, tp, lay.width)
        hm, c_p, n_p, m_p = _mlstm(z3, lay, CHUNK, bias_row, bias_col, m_norm3, l, zero_c, zero_n, zero_m, 0)
        lf_p, crow = _foxcum(z3, lay, CHUNK, bias_row, bias_col, l)
        hf = _flash(z3, crow, lay, tq)
        ysd, h_p = _ssd(z3, lay, CHUNK, bias_row, bias_col, alog_row, alog_col, s_conv_w, conv_b3, d_exp, s_norm3,
                        l, zero_h, 0, zero_cv, ns, s_hd)
        mixed = _merge(hm.reshape(np_, mix), hf.reshape(np_, mix), ysd.reshape(np_, mix), wb, z, lay, l, tm_p, 512)
        xp = _outproj(mixed, wo, g_mix_post, xp, l, _row_tile(np_, 512))
        xp = _mlp(xp, g_mlp_pre, wu, wd, g_mlp_post, l, _row_tile(np_, 512), 512)
        fk_o, fv_o, u_o = lay.dst["fk"], lay.dst["fv"], lay.dst["sxbc"]
        for lst, a in zip(outs_p, (z3[:, :, fk_o:fk_o + mix].reshape(bp, tp, fh, fhd),
                                   z3[:, :, fv_o:fv_o + mix].reshape(bp, tp, fh, fhd),
                                   lf_p, c_p, n_p.reshape(bp, mh, dk), m_p.reshape(bp, mh),
                                   h_p.reshape(bp, sh, s_hd, ns), z3[:, tp - (kconv - 1):, u_o:u_o + scd])):
            lst.append(a)

        zs = _inproj(xs, g_mix_pre, w_in_p, l, tm_s, tn_in)
        zs3 = zs.reshape(bs, ts, lay.width)
        zs8 = jnp.pad(zs3, ((0, 0), (0, tsp - ts), (0, 0)))
        hm_s, c_s, n_s, m_s = _mlstm(zs8, lay, ts, bias_row, bias_col, m_norm3, l, c0_s, n0_s, m0_s, l)
        lf_s, crow_s = _foxcum(zs8, lay, ts, bias_row, bias_col, l)
        fnew = jnp.transpose(crow_s[:, :, :ts], (0, 2, 1)).reshape(bs, 1, ts * fh)
        fnew = jnp.pad(fnew, ((0, 0), (0, 0), (0, LANES - ts * fh)))
        z5 = zs8.reshape(bs, tsp, lay.width // LANES, LANES)
        hf_s = _decode(z5, fnew, cache_k, cache_v, rt, page_table, lay, l, ts, npg)
        cv0 = jnp.pad(state_conv[l], ((0, 0), (SUBLANES - (kconv - 1), 0), (0, 0)))
        ys_s, h_s = _ssd(zs8, lay, ts, bias_row, bias_col, alog_row, alog_col, s_conv_w, conv_b3, d_exp, s_norm3,
                         l, h0_s, l, cv0, ns, s_hd)
        mixed_s = _merge(hm_s[:, :ts].reshape(ns_, mix), hf_s.reshape(ns_, mix).astype(BF16),
                         ys_s[:, :ts].reshape(ns_, mix), wb, zs, lay, l, tm_s, 512)
        xs = _outproj(mixed_s, wo, g_mix_post, xs, l, tm_s)
        xs = _mlp(xs, g_mlp_pre, wu, wd, g_mlp_post, l, tm_s, 512)
        conv_full = jnp.concatenate([state_conv[l], zs3[:, :, u_o:u_o + scd]], axis=1)
        for lst, a in zip(outs_s, (zs3[:, :, fk_o:fk_o + mix].reshape(bs, ts, fh, fhd),
                                   zs3[:, :, fv_o:fv_o + mix].reshape(bs, ts, fh, fhd),
                                   lf_s[:, :ts], c_s, n_s.reshape(bs, mh, dk), m_s.reshape(bs, mh),
                                   h_s.reshape(bs, sh, s_hd, ns), conv_full[:, -(kconv - 1):])):
            lst.append(a)

    k_p, v_p, lf_pp, c_pp, n_pp, m_pp, ssd_p, conv_p = [jnp.stack(a) for a in outs_p]
    k_s, v_s, lf_ss, c_ss, n_ss, m_ss, ssd_s, conv_s = [jnp.stack(a) for a in outs_s]
    return (xp.reshape(bp, tp, d), xs.reshape(bs, ts, d), k_p, v_p, lf_pp, k_s, v_s, lf_ss,
            c_pp, n_pp, m_pp, c_ss, n_ss, m_ss, ssd_p, ssd_s, conv_p, conv_s)
```

```python
import functools

import jax
import jax.numpy as jnp
from jax import lax
from jax.experimental import pallas as pl
from jax.experimental.pallas import tpu as pltpu

F32 = jnp.float32
BF16 = jnp.bfloat16
HI = lax.Precision.HIGHEST
EPS = 1e-6
NEG_INF = float("-inf")
LANES = 128
SUBLANES = 8
CHUNK = 128
VMEM_LIMIT = 56 * 1024 * 1024


def _cp(*sem, vmem=VMEM_LIMIT):
    return pltpu.CompilerParams(dimension_semantics=sem, vmem_limit_bytes=vmem)


def _nt(a, b):
    return lax.dot_general(a, b, (((1,), (1,)), ((), ())), preferred_element_type=F32)


def _mm(a, b):
    return jnp.dot(a, b, preferred_element_type=F32)


def _mm_exact(a, b):
    return jnp.dot(a, b, preferred_element_type=F32, precision=HI)


def _softplus(x):
    return jnp.maximum(x, 0.0) + jnp.log1p(jnp.exp(-jnp.abs(x)))


def _log_sigmoid(x):
    return -_softplus(-x)


def _sigmoid(x):
    return 1.0 / (1.0 + jnp.exp(-x))


def _silu(x):
    return x * _sigmoid(x)


def _rms(x, g):
    return x * lax.rsqrt(jnp.mean(x * x, axis=-1, keepdims=True) + EPS) * g


def _pad_rows(a, rows):
    if a.shape[0] == rows:
        return a
    return jnp.concatenate([a, jnp.zeros((rows - a.shape[0], a.shape[1]), a.dtype)], axis=0)


def _imod(x, n):
    assert n & (n - 1) == 0
    return x & (n - 1)


def _idiv(x, n):
    assert n & (n - 1) == 0
    return x >> (n.bit_length() - 1)


def _pick_col(a, sel):
    return jnp.sum(jnp.where(sel, a, 0.0), axis=1, keepdims=True)


class _Layout:
    def __init__(self, d_model, mix, mh, fh, sh, scd):
        self.d, self.mix, self.mh, self.fh, self.sh, self.scd = d_model, mix, mh, fh, sh, scd
        self.dk = mix // mh // 2
        self.dv = mix // mh
        self.fhd = mix // fh
        src = [("mq", mh * self.dk), ("mk", mh * self.dk), ("mv", mix), ("mi", mh), ("mf", mh), ("mo", mix),
               ("fq", mix), ("fk", mix), ("fv", mix), ("ff", fh), ("sz", mix), ("sxbc", scd), ("sdt", sh),
               ("g", 3 * d_model)]
        self.src = {}
        off = 0
        for name, w in src:
            self.src[name] = (off, w)
            off += w
        self.in_width = off
        order = ["mq", "mk", "mv", "sxbc", "mo", "fq", "fk", "fv", "sz", "g"]
        self.dst = {}
        off = 0
        for name in order:
            self.dst[name] = off
            off += self.src[name][1]
        self.small = off
        self.small_order = ["ff", "mi", "mf", "sdt"]
        self.lane = {}
        l = 0
        for name in self.small_order:
            self.lane[name] = l
            l += self.src[name][1]
        assert l <= LANES
        self.small_used = l
        self.order = order
        self.width = off + LANES


def _pack_plan(lay):
    plan = []
    for name in lay.order:
        src, width = lay.src[name]
        for kblk in range(width // LANES):
            start = src + kblk * LANES
            plan.append((start // LANES, start % LANES))
    return plan


def _pack_kernel(w_ref, tail_ref, o_ref, *, plan, small, n_src_blocks):
    lane = lax.broadcasted_iota(jnp.int32, (1, LANES), 1)

    def src_block(b):
        if b == n_src_blocks:
            return tail_ref[...]
        return w_ref[:, b * LANES:(b + 1) * LANES]

    for ob, (sb, shift) in enumerate(plan):
        lo = src_block(sb)
        if shift == 0:
            out = lo
        else:
            hi = src_block(sb + 1)
            out = jnp.where(lane < LANES - shift, pltpu.roll(lo, LANES - shift, 1), pltpu.roll(hi, LANES - shift, 1))
        o_ref[:, ob * LANES:(ob + 1) * LANES] = out.astype(BF16)
    acc = jnp.zeros((w_ref.shape[0], LANES), F32)
    for src, width, dst in small:
        blk = src_block(src // LANES)
        off = src % LANES
        moved = blk if off == dst else pltpu.roll(blk, (dst - off) % LANES, 1)
        acc = jnp.where((lane >= dst) & (lane < dst + width), moved, acc)
    o_ref[:, len(plan) * LANES:(len(plan) + 1) * LANES] = acc.astype(BF16)


def _pack_w_in(w_in, lay, rows):
    depth, d, width = w_in.shape
    n_full = width // LANES
    tail = jnp.pad(w_in[:, :, n_full * LANES:], ((0, 0), (0, 0), (0, (n_full + 1) * LANES - width)))
    small = [(lay.src[n][0], lay.src[n][1], lay.lane[n]) for n in lay.small_order]
    for src, w, _ in small:
        assert src // LANES == (src + w - 1) // LANES
    return pl.pallas_call(
        functools.partial(_pack_kernel, plan=_pack_plan(lay), small=small, n_src_blocks=n_full),
        grid=(depth, d // rows),
        in_specs=[pl.BlockSpec((None, rows, width), lambda l, i: (l, i, 0)),
                  pl.BlockSpec((None, rows, LANES), lambda l, i: (l, i, 0))],
        out_specs=pl.BlockSpec((None, rows, lay.width), lambda l, i: (l, i, 0)),
        out_shape=jax.ShapeDtypeStruct((depth, d, lay.width), BF16),
        compiler_params=_cp("parallel", "parallel"),
        name="pack_w_in",
    )(w_in, tail)


def _inproj_kernel(x_ref, g_ref, w_ref, o_ref, h_ref):
    @pl.when(pl.program_id(1) == 0)
    def _():
        h_ref[...] = _rms(x_ref[...], g_ref[...]).astype(BF16)

    o_ref[...] = _mm(h_ref[...], w_ref[...])


def _inproj(x, g, w, l, tm, tn):
    n, d = x.shape
    width = w.shape[-1]
    return pl.pallas_call(
        _inproj_kernel,
        grid=(n // tm, width // tn),
        in_specs=[pl.BlockSpec((tm, d), lambda i, j: (i, 0)),
                  pl.BlockSpec((None, 1, d), lambda i, j: (l, 0, 0)),
                  pl.BlockSpec((None, d, tn), lambda i, j: (l, 0, j))],
        out_specs=pl.BlockSpec((tm, tn), lambda i, j: (i, j)),
        out_shape=jax.ShapeDtypeStruct((n, width), F32),
        scratch_shapes=[pltpu.VMEM((tm, d), BF16)],
        compiler_params=_cp("parallel", "arbitrary"),
        name="inproj",
    )(x, g, w)


def _mlstm_kernel(q_ref, k_ref, v_ref, og_ref, s_ref, brow_ref, bcol_ref, mn_ref, c0_ref, n0_ref, m0_ref,
                  hm_ref, cout_ref, nout_ref, mout_ref, lf_ref, crow_ref, ct_ref, n_ref, m_ref, fcarry_ref, *,
                  L, lr, nh, dk, dv, lane_i, lane_f, lane_ff, nfh, kscale):
    c = pl.program_id(1)
    nc = pl.num_programs(1)

    @pl.when(c == 0)
    def _():
        for h in range(nh):
            ct_ref[h] = c0_ref[0, h].T
        n_ref[...] = n0_ref[0]
        m_ref[...] = m0_ref[0]
        fcarry_ref[...] = jnp.zeros_like(fcarry_ref)

    gates = _pad_rows(s_ref[0], L)
    row = lax.broadcasted_iota(jnp.int32, (L, 1), 0)
    col = lax.broadcasted_iota(jnp.int32, (1, L), 1)
    lane = lax.broadcasted_iota(jnp.int32, (1, LANES), 1)
    valid_c = row < lr
    valid_r = col < lr
    causal = col <= row
    gb = gates + brow_ref[...]
    gt = gates.T + bcol_ref[...]
    lf = jnp.where(valid_c, _log_sigmoid(gb), 0.0)
    lft = jnp.where(valid_r, _log_sigmoid(gt), 0.0)
    bc_all = _mm_exact(causal.astype(F32), lf)
    br_all = _mm_exact(lft, (row <= col).astype(F32))
    rows_out = hm_ref.shape[1]

    lf_ref[0] = lf[:rows_out, lane_ff:lane_ff + nfh]
    csum = br_all[lane_ff:lane_ff + nfh] + fcarry_ref[...]
    crow_ref[0] = csum
    fcarry_ref[...] = csum[:, L - 1:L]

    for h in range(nh):
        q = _pad_rows(q_ref[0, :, h * dk:(h + 1) * dk], L)
        k = _pad_rows(k_ref[0, :, h * dk:(h + 1) * dk], L) * kscale
        v = _pad_rows(v_ref[0, :, h * dv:(h + 1) * dv], L)
        i_c = jnp.where(valid_c, _pick_col(gb, lane == lane_i + h), NEG_INF)
        i_r = jnp.where(valid_r, gt[lane_i + h:lane_i + h + 1, :], NEG_INF)
        b_c = _pick_col(bc_all, lane == lane_f + h)
        b_r = br_all[lane_f + h:lane_f + h + 1, :]

        m_prev = m_ref[h]
        log_d = jnp.where(causal, b_c + (i_r - b_r), NEG_INF)
        inter = b_c + m_prev
        m_t = jnp.maximum(inter, jnp.max(log_d, axis=1, keepdims=True))
        w_inter = jnp.exp(inter - m_t)
        qb = q.astype(BF16)
        kb = k.astype(BF16)
        vb = v.astype(BF16)
        s = _nt(qb, kb) * jnp.exp(log_d - m_t)
        ct = ct_ref[h]
        nvec = n_ref[h]
        num = _mm(s.astype(BF16), vb) + w_inter * _mm(qb, ct.astype(BF16))
        nq = jnp.sum(qb.astype(F32) * nvec.astype(BF16).astype(F32), axis=1, keepdims=True)
        den = jnp.sum(s, axis=1, keepdims=True) + w_inter * nq
        hh = num / jnp.maximum(jnp.abs(den), jnp.exp(-m_t))

        b_last = b_c[L - 1:L, :]
        le_c = b_last - b_c + i_c
        le_r = b_last - b_r + i_r
        m_new = jnp.maximum(b_last + m_prev, jnp.max(le_r, axis=1, keepdims=True))
        we_c = jnp.exp(le_c - m_new)
        we_r = jnp.exp(le_r - m_new)
        ws = jnp.exp(b_last + m_prev - m_new)
        ct_ref[h] = ws * ct + _mm(k.T.astype(BF16), (we_c * v).astype(BF16))
        n_ref[h] = ws * nvec + _mm(jnp.broadcast_to(we_r, (SUBLANES, L)).astype(BF16), kb)[0:1]
        m_ref[h] = m_new

        y = _rms(hh, mn_ref[:, h * dv:(h + 1) * dv]) * _sigmoid(_pad_rows(og_ref[0, :, h * dv:(h + 1) * dv], L))
        hm_ref[0, :, h * dv:(h + 1) * dv] = y[:rows_out].astype(BF16)

    @pl.when(c == nc - 1)
    def _():
        for h in range(nh):
            cout_ref[0, h] = ct_ref[h].T
        nout_ref[0] = n_ref[...]
        mout_ref[0] = m_ref[...]


def _mlstm(z3, lay, lr, bias_row, bias_col, m_norm, l, c0, n0, m0, ls):
    bsz, t, _ = z3.shape
    L = CHUNK
    rb = min(t, L)
    nc = t // rb
    mh, dk, dv = lay.mh, lay.dk, lay.dv
    fh = lay.fh
    kern = functools.partial(_mlstm_kernel, L=L, lr=lr, nh=mh, dk=dk, dv=dv, lane_i=lay.lane["mi"],
                             lane_f=lay.lane["mf"], lane_ff=lay.lane["ff"], nfh=fh, kscale=float(dk) ** -0.5)
    qw, vw = mh * dk, mh * dv
    qo, ko, vo, oo, so = (lay.dst["mq"] // qw, lay.dst["mk"] // qw, lay.dst["mv"] // vw, lay.dst["mo"] // vw,
                          lay.small // LANES)
    return pl.pallas_call(
        kern,
        grid=(bsz, nc),
        in_specs=[pl.BlockSpec((1, rb, qw), lambda b, c: (b, c, qo)),
                  pl.BlockSpec((1, rb, qw), lambda b, c: (b, c, ko)),
                  pl.BlockSpec((1, rb, vw), lambda b, c: (b, c, vo)),
                  pl.BlockSpec((1, rb, vw), lambda b, c: (b, c, oo)),
                  pl.BlockSpec((1, rb, LANES), lambda b, c: (b, c, so)),
                  pl.BlockSpec((None, 1, LANES), lambda b, c: (l, 0, 0)),
                  pl.BlockSpec((None, LANES, 1), lambda b, c: (l, 0, 0)),
                  pl.BlockSpec((None, 1, vw), lambda b, c: (l, 0, 0)),
                  pl.BlockSpec((None, 1, mh, dv, dk), lambda b, c: (ls, b, 0, 0, 0)),
                  pl.BlockSpec((None, 1, mh, 1, dk), lambda b, c: (ls, b, 0, 0, 0)),
                  pl.BlockSpec((None, 1, mh, 1, 1), lambda b, c: (ls, b, 0, 0, 0))],
        out_specs=[pl.BlockSpec((1, rb, vw), lambda b, c: (b, c, 0)),
                   pl.BlockSpec((1, mh, dv, dk), lambda b, c: (b, 0, 0, 0)),
                   pl.BlockSpec((1, mh, 1, dk), lambda b, c: (b, 0, 0, 0)),
                   pl.BlockSpec((1, mh, 1, 1), lambda b, c: (b, 0, 0, 0)),
                   pl.BlockSpec((1, rb, fh), lambda b, c: (b, c, 0)),
                   pl.BlockSpec((1, fh, L), lambda b, c: (b, 0, c))],
        out_shape=[jax.ShapeDtypeStruct((bsz, t, vw), BF16),
                   jax.ShapeDtypeStruct((bsz, mh, dv, dk), F32),
                   jax.ShapeDtypeStruct((bsz, mh, 1, dk), F32),
                   jax.ShapeDtypeStruct((bsz, mh, 1, 1), F32),
                   jax.ShapeDtypeStruct((bsz, t, fh), F32),
                   jax.ShapeDtypeStruct((bsz, fh, nc * L), F32)],
        scratch_shapes=[pltpu.VMEM((mh, dk, dv), F32), pltpu.VMEM((mh, 1, dk), F32), pltpu.VMEM((mh, 1, 1), F32),
                        pltpu.VMEM((fh, 1), F32)],
        compiler_params=_cp("parallel", "arbitrary"),
        name="mlstm",
    )(z3, z3, z3, z3, z3, bias_row, bias_col, m_norm, c0, n0, m0)


def _flash_kernel(q_ref, k_ref, v_ref, c_ref, o_ref, kb_ref, vb_ref, *, tq, scale):
    qi = pl.program_id(2)

    @pl.when(qi == 0)
    def _():
        kb_ref[...] = k_ref[0].astype(BF16)
        vb_ref[...] = v_ref[0].astype(BF16)

    q = q_ref[0].astype(BF16)
    hd = q.shape[1]
    causal = lax.broadcasted_iota(jnp.int32, (1, tq), 1) <= lax.broadcasted_iota(jnp.int32, (tq, 1), 0)

    def block(j, carry, diagonal):
        m_prev, l_prev, acc = carry
        off = pl.multiple_of(j * tq, tq)
        s = _nt(q, kb_ref[pl.ds(off, tq), :]) * scale - c_ref[0, 0, j]
        if diagonal:
            s = jnp.where(causal, s, NEG_INF)
        m_new = jnp.maximum(m_prev, jnp.max(s, axis=1, keepdims=True))
        alpha = jnp.exp(m_prev - m_new)
        p = jnp.exp(s - m_new)
        l_new = alpha * l_prev + jnp.sum(p, axis=1, keepdims=True)
        acc = alpha * acc + _mm(p.astype(BF16), vb_ref[pl.ds(off, tq), :])
        return m_new, l_new, acc

    init = (jnp.full((tq, 1), NEG_INF, F32), jnp.zeros((tq, 1), F32), jnp.zeros((tq, hd), F32))
    carry = lax.fori_loop(0, qi, lambda j, c: block(j, c, False), init)
    _, l_fin, acc = block(qi, carry, True)
    o_ref[0] = (acc / l_fin).astype(BF16)


def _flash(z3, crow, lay, tq):
    bsz, t, _ = z3.shape
    fh, hd = lay.fh, lay.fhd
    qo, ko, vo = lay.dst["fq"] // hd, lay.dst["fk"] // hd, lay.dst["fv"] // hd
    nq = t // tq
    c5 = crow.reshape(bsz, fh, nq, 1, tq)
    return pl.pallas_call(
        functools.partial(_flash_kernel, tq=tq, scale=float(hd) ** -0.5),
        grid=(bsz, fh, nq),
        in_specs=[pl.BlockSpec((1, tq, hd), lambda b, h, qi: (b, qi, qo + h)),
                  pl.BlockSpec((1, t, hd), lambda b, h, qi: (b, 0, ko + h)),
                  pl.BlockSpec((1, t, hd), lambda b, h, qi: (b, 0, vo + h)),
                  pl.BlockSpec((1, 1, nq, 1, tq), lambda b, h, qi: (b, h, 0, 0, 0))],
        out_specs=pl.BlockSpec((1, tq, hd), lambda b, h, qi: (b, qi, h)),
        out_shape=jax.ShapeDtypeStruct((bsz, t, fh * hd), BF16),
        scratch_shapes=[pltpu.VMEM((t, hd), BF16), pltpu.VMEM((t, hd), BF16)],
        compiler_params=_cp("parallel", "parallel", "arbitrary"),
        name="fox_flash",
    )(z3, z3, z3, c5)


def _lfsuffix_kernel(x_ref, o_ref, mat_ref, *, nh):
    w = x_ref.shape[1]

    @pl.when(pl.program_id(0) == 0)
    def _():
        ri = lax.broadcasted_iota(jnp.int32, (w, 1), 0)
        ci = lax.broadcasted_iota(jnp.int32, (1, w), 1)
        same = _imod(ri, nh) == _imod(ci, nh)
        mat_ref[:, 0:w] = jnp.where(same & (_idiv(ri, nh) > _idiv(ci, nh)), 1.0, 0.0).astype(BF16)
        mat_ref[:, w:2 * w] = jnp.where(same, 1.0, 0.0).astype(BF16)

    x = x_ref[...]
    x1 = x.astype(BF16)
    r1 = x - x1.astype(F32)
    x2 = r1.astype(BF16)
    x3 = (r1 - x2.astype(F32)).astype(BF16)
    mat = mat_ref[...]
    o_ref[...] = _mm(x1, mat) + (_mm(x2, mat) + _mm(x3, mat))


def _lfsuffix(lf_flat, nh, tm):
    n, w = lf_flat.shape
    return pl.pallas_call(
        functools.partial(_lfsuffix_kernel, nh=nh),
        grid=(n // tm,),
        in_specs=[pl.BlockSpec((tm, w), lambda i: (i, 0))],
        out_specs=pl.BlockSpec((tm, 2 * w), lambda i: (i, 0)),
        out_shape=jax.ShapeDtypeStruct((n, 2 * w), F32),
        scratch_shapes=[pltpu.VMEM((w, 2 * w), BF16)],
        compiler_params=_cp("arbitrary"),
        name="lf_suffix",
    )(lf_flat)


def _decode_kernel(pt_ref, q_ref, kn_ref, vn_ref, fn_ref, *refs, scale, nq, nh, npg):
    k_refs, v_refs, rt_refs = refs[0:npg], refs[npg:2 * npg], refs[2 * npg:3 * npg]
    o_ref, m_ref, l_ref, acc_ref, tail_ref = refs[3 * npg:]
    j = pl.program_id(1)
    nj = pl.num_programs(1)
    rows = nq * nh
    hd = q_ref.shape[-1]
    w = k_refs[0].shape[0] * k_refs[0].shape[1]

    @pl.when(j == 0)
    def _():
        m_ref[...] = jnp.full_like(m_ref, NEG_INF)
        l_ref[...] = jnp.zeros_like(l_ref)
        acc_ref[...] = jnp.zeros_like(acc_ref)
        tail_ref[...] = jnp.zeros_like(tail_ref)

    qs = q_ref[0].reshape(rows, hd).astype(BF16)
    row_h = _imod(lax.broadcasted_iota(jnp.int32, (rows, 1), 0), nh)

    def update(scores, values):
        m_prev = m_ref[...]
        smax = scores[0]
        for s in scores[1:]:
            smax = jnp.maximum(smax, s)
        m_new = jnp.maximum(m_prev, jnp.max(smax, axis=1, keepdims=True))
        alpha = jnp.exp(m_prev - m_new)
        psum = None
        pv = None
        for s, vb in zip(scores, values):
            p = jnp.exp(s - m_new)
            psum = p if psum is None else psum + p
            d = _mm(p.astype(BF16), vb)
            pv = d if pv is None else pv + d
        l_ref[...] = alpha * l_ref[...] + jnp.sum(psum, axis=1, keepdims=True)
        acc_ref[...] = alpha * acc_ref[...] + pv
        m_ref[...] = m_new

    same_head = _imod(lax.broadcasted_iota(jnp.int32, (1, w), 1), nh) == row_h
    tail = tail_ref[...]
    scores, values = [], []
    for i in range(npg):
        kf = k_refs[i][...].reshape(w, hd).astype(BF16)
        rt = rt_refs[i][0]
        s = _nt(qs, kf) * scale + (rt[:, 0:w] + tail)
        scores.append(jnp.where(same_head, s, NEG_INF))
        values.append(v_refs[i][...].reshape(w, hd).astype(BF16))
        tail = tail + rt[:, w:2 * w]
    tail_ref[...] = tail
    update(scores, values)

    @pl.when(j == nj - 1)
    def _():
        kn = _pad_rows(kn_ref[0].reshape(rows, hd), LANES).astype(BF16)
        vn = _pad_rows(vn_ref[0].reshape(rows, hd), LANES).astype(BF16)
        lane = lax.broadcasted_iota(jnp.int32, (1, LANES), 1)
        row_q = _idiv(lax.broadcasted_iota(jnp.int32, (rows, 1), 0), nh)
        ok = (_imod(lane, nh) == row_h) & (_idiv(lane, nh) <= row_q) & (lane < rows)
        s_new = _nt(qs, kn) * scale - fn_ref[0]
        update([jnp.where(ok, s_new, NEG_INF)], [vn])
        o_ref[0] = (acc_ref[...] / l_ref[...]).reshape(nq, nh, hd)


def _decode(z5, fnew, cache_k, cache_v, rt, page_table, lay, l, nq, npg):
    bsz = z5.shape[0]
    fh, hd = lay.fh, lay.fhd
    n_pool, page = cache_k.shape[1], cache_k.shape[2]
    n_pages = page_table.shape[1]
    w = page * fh
    qo, ko, vo = lay.dst["fq"] // (fh * hd), lay.dst["fk"] // (fh * hd), lay.dst["fv"] // (fh * hd)

    def kv_spec(i):
        return pl.BlockSpec((None, None, page, fh, hd),
                            lambda b, j, pt: (l, pt[b, n_pages - 1 - (j * npg + i)], 0, 0, 0))

    def rt_spec(i):
        return pl.BlockSpec((1, 1, 2 * w), lambda b, j, pt: (l * n_pool + pt[b, n_pages - 1 - (j * npg + i)], 0, 0))

    grid_spec = pltpu.PrefetchScalarGridSpec(
        num_scalar_prefetch=1,
        grid=(bsz, n_pages // npg),
        in_specs=[pl.BlockSpec((1, nq, fh, hd), lambda b, j, pt: (b, 0, qo, 0)),
                  pl.BlockSpec((1, nq, fh, hd), lambda b, j, pt: (b, 0, ko, 0)),
                  pl.BlockSpec((1, nq, fh, hd), lambda b, j, pt: (b, 0, vo, 0)),
                  pl.BlockSpec((1, 1, LANES), lambda b, j, pt: (b, 0, 0))]
                 + [kv_spec(i) for i in range(npg)] + [kv_spec(i) for i in range(npg)]
                 + [rt_spec(i) for i in range(npg)],
        out_specs=pl.BlockSpec((1, nq, fh, hd), lambda b, j, pt: (b, 0, 0, 0)),
        scratch_shapes=[pltpu.VMEM((nq * fh, 1), F32), pltpu.VMEM((nq * fh, 1), F32),
                        pltpu.VMEM((nq * fh, hd), F32), pltpu.VMEM((1, w), F32)],
    )
    return pl.pallas_call(
        functools.partial(_decode_kernel, scale=float(hd) ** -0.5, nq=nq, nh=fh, npg=npg),
        grid_spec=grid_spec,
        out_shape=jax.ShapeDtypeStruct((bsz, nq, fh, hd), F32),
        compiler_params=_cp("parallel", "arbitrary"),
        name="fox_decode",
    )(page_table, z5, z5, z5, fnew, *([cache_k] * npg), *([cache_v] * npg), *([rt] * npg))


def _ssd_kernel(u_ref, sz_ref, s_ref, brow_ref, bcol_ref, alrow_ref, alcol_ref, cw_ref, cb_ref, dexp_ref, sn_ref,
                h0_ref, cv0_ref, y_ref, hout_ref, xs_ref, ht_ref, *, L, lr, ng, hpg, hd, ns, lane_dt, kconv):
    c = pl.program_id(1)
    nc = pl.num_programs(1)
    mix = ng * hpg * hd
    gw = hpg * hd

    @pl.when(c == 0)
    def _():
        xs_ref[0:SUBLANES, :] = cv0_ref[0]
        for g in range(ng):
            ht_ref[g] = h0_ref[0, g * gw:(g + 1) * gw, :].T

    @pl.when(c > 0)
    def _():
        xs_ref[0:SUBLANES, :] = xs_ref[L:L + SUBLANES, :]

    xs_ref[SUBLANES:SUBLANES + L, :] = _pad_rows(u_ref[0], L)
    conv = cb_ref[...]
    for jj in range(kconv):
        off = SUBLANES - (kconv - 1) + jj
        conv = conv + cw_ref[jj:jj + 1, :] * xs_ref[pl.ds(off, L), :]
    xbc = _silu(conv)
    x = xbc[:, :mix]

    gates = _pad_rows(s_ref[0], L)
    row = lax.broadcasted_iota(jnp.int32, (L, 1), 0)
    col = lax.broadcasted_iota(jnp.int32, (1, L), 1)
    lane = lax.broadcasted_iota(jnp.int32, (1, LANES), 1)
    causal = col <= row
    dt_c = jnp.where(row < lr, _softplus(gates + brow_ref[...]), 0.0)
    dt_r = jnp.where(col < lr, _softplus(gates.T + bcol_ref[...]), 0.0)
    cum_c = _mm_exact(causal.astype(F32), dt_c * (-jnp.exp(alrow_ref[...])))
    cum_r = _mm_exact(dt_r * (-jnp.exp(alcol_ref[...])), (row <= col).astype(F32))
    lane_grp = _idiv(lax.broadcasted_iota(jnp.int32, (1, gw), 1), hd)

    ys = []
    for g in range(ng):
        bg = xbc[:, mix + g * ns:mix + (g + 1) * ns]
        cg = xbc[:, mix + (ng + g) * ns:mix + (ng + g + 1) * ns]
        bb = bg.astype(BF16)
        cb = cg.astype(BF16)
        cbm = _nt(cb, bb)
        xg = x[:, g * gw:(g + 1) * gw]
        ht = ht_ref[g]
        y = jnp.zeros((L, gw), F32)
        ecum = jnp.zeros((L, gw), F32)
        wend = jnp.zeros((L, gw), F32)
        dec = jnp.zeros((1, gw), F32)
        for r in range(hpg):
            idx = lane_dt + g * hpg + r
            cc = _pick_col(cum_c, lane == idx)
            dtc = _pick_col(dt_c, lane == idx)
            cr = cum_r[idx:idx + 1, :]
            dtr = dt_r[idx:idx + 1, :]
            wmat = cbm * jnp.exp(jnp.where(causal, cc - cr, NEG_INF)) * dtr
            sel = lane_grp == r
            y = y + _mm(wmat.astype(BF16), jnp.where(sel, xg, 0.0).astype(BF16))
            ce = cc[L - 1:L, :]
            ecum = jnp.where(sel, jnp.exp(cc), ecum)
            wend = jnp.where(sel, jnp.exp(ce - cc) * dtc, wend)
            dec = jnp.where(sel, jnp.exp(ce), dec)
        y = y + ecum * _mm(cb, ht.astype(BF16))
        ht_ref[g] = dec * ht + _mm(bg.T.astype(BF16), (wend * xg).astype(BF16))
        ys.append(y + dexp_ref[:, g * gw:(g + 1) * gw] * xg)
    yall = jnp.concatenate(ys, axis=1) * _silu(_pad_rows(sz_ref[0], L))
    y_ref[0] = _rms(yall, sn_ref[...])[:y_ref.shape[1]].astype(BF16)

    @pl.when(c == nc - 1)
    def _():
        for g in range(ng):
            hout_ref[0, g * gw:(g + 1) * gw, :] = ht_ref[g].T


def _ssd(z3, lay, lr, bias_row, bias_col, alog_row, alog_col, conv_w, conv_b, d_exp, s_norm, l, h0, ls, cv0, ns, hd):
    bsz, t, _ = z3.shape
    L = CHUNK
    rb = min(t, L)
    nc = t // rb
    mix, scd, sh = lay.mix, lay.scd, lay.sh
    ng = (scd - mix) // (2 * ns)
    hpg = sh // ng
    kconv = conv_w.shape[1]
    kern = functools.partial(_ssd_kernel, L=L, lr=lr, ng=ng, hpg=hpg, hd=hd, ns=ns, lane_dt=lay.lane["sdt"],
                             kconv=kconv)
    uo, zo, so = lay.dst["sxbc"] // scd, lay.dst["sz"] // mix, lay.small // LANES
    return pl.pallas_call(
        kern,
        grid=(bsz, nc),
        in_specs=[pl.BlockSpec((1, rb, scd), lambda b, c: (b, c, uo)),
                  pl.BlockSpec((1, rb, mix), lambda b, c: (b, c, zo)),
                  pl.BlockSpec((1, rb, LANES), lambda b, c: (b, c, so)),
                  pl.BlockSpec((None, 1, LANES), lambda b, c: (l, 0, 0)),
                  pl.BlockSpec((None, LANES, 1), lambda b, c: (l, 0, 0)),
                  pl.BlockSpec((None, 1, LANES), lambda b, c: (l, 0, 0)),
                  pl.BlockSpec((None, LANES, 1), lambda b, c: (l, 0, 0)),
                  pl.BlockSpec((None, kconv, scd), lambda b, c: (l, 0, 0)),
                  pl.BlockSpec((None, 1, scd), lambda b, c: (l, 0, 0)),
                  pl.BlockSpec((None, 1, mix), lambda b, c: (l, 0, 0)),
                  pl.BlockSpec((None, 1, mix), lambda b, c: (l, 0, 0)),
                  pl.BlockSpec((None, 1, sh * hd, ns), lambda b, c: (ls, b, 0, 0)),
                  pl.BlockSpec((1, SUBLANES, scd), lambda b, c: (b, 0, 0))],
        out_specs=[pl.BlockSpec((1, rb, mix), lambda b, c: (b, c, 0)),
                   pl.BlockSpec((1, sh * hd, ns), lambda b, c: (b, 0, 0))],
        out_shape=[jax.ShapeDtypeStruct((bsz, t, mix), BF16),
                   jax.ShapeDtypeStruct((bsz, sh * hd, ns), F32)],
        scratch_shapes=[pltpu.VMEM((L + 2 * SUBLANES, scd), F32), pltpu.VMEM((ng, ns, hpg * hd), F32)],
        compiler_params=_cp("parallel", "arbitrary"),
        name="ssd",
    )(z3, z3, z3, bias_row, bias_col, alog_row, alog_col, conv_w, conv_b, d_exp, s_norm, h0, cv0)


def _merge_kernel(hm_ref, hf_ref, ys_ref, wb_ref, g0_ref, g1_ref, g2_ref, o_ref):
    acc = _sigmoid(g0_ref[...]) * _mm(hm_ref[...], wb_ref[0])
    acc = acc + _sigmoid(g1_ref[...]) * _mm(hf_ref[...], wb_ref[1])
    acc = acc + _sigmoid(g2_ref[...]) * _mm(ys_ref[...], wb_ref[2])
    o_ref[...] = acc.astype(BF16)


def _merge(hm, hf, ys, wb, z, lay, l, tm, tn):
    n, mix = hm.shape
    d = lay.d
    go = lay.dst["g"] // tn
    per = d // tn
    gspec = lambda k: pl.BlockSpec((tm, tn), lambda i, j: (i, go + k * per + j))
    act = pl.BlockSpec((tm, mix), lambda i, j: (i, 0))
    return pl.pallas_call(
        _merge_kernel,
        grid=(n // tm, d // tn),
        in_specs=[act, act, act, pl.BlockSpec((None, 3, mix, tn), lambda i, j: (l, 0, 0, j)),
                  gspec(0), gspec(1), gspec(2)],
        out_specs=pl.BlockSpec((tm, tn), lambda i, j: (i, j)),
        out_shape=jax.ShapeDtypeStruct((n, d), BF16),
        compiler_params=_cp("parallel", "arbitrary"),
        name="merge",
    )(hm, hf, ys, wb, z, z, z)


def _outproj_kernel(a_ref, w_ref, g_ref, x_ref, o_ref):
    o_ref[...] = x_ref[...] + _rms(_mm(a_ref[...], w_ref[...]), g_ref[...])


def _outproj(a, w, g, x, l, tm):
    n, d = x.shape
    return pl.pallas_call(
        _outproj_kernel,
        grid=(n // tm,),
        in_specs=[pl.BlockSpec((tm, d), lambda i: (i, 0)),
                  pl.BlockSpec((None, d, d), lambda i: (l, 0, 0)),
                  pl.BlockSpec((None, 1, d), lambda i: (l, 0, 0)),
                  pl.BlockSpec((tm, d), lambda i: (i, 0))],
        out_specs=pl.BlockSpec((tm, d), lambda i: (i, 0)),
        out_shape=jax.ShapeDtypeStruct((n, d), F32),
        compiler_params=_cp("parallel"),
        name="outproj",
    )(a, w, g, x)


def _mlp_kernel(x_ref, gpre_ref, wu_ref, wd_ref, gpost_ref, o_ref, h_ref):
    j = pl.program_id(1)

    @pl.when(j == 0)
    def _():
        h_ref[...] = _rms(x_ref[...], gpre_ref[...]).astype(BF16)
        o_ref[...] = jnp.zeros_like(o_ref)

    u = jnp.maximum(_mm(h_ref[...], wu_ref[...]), 0.0)
    o_ref[...] += _mm((u * u).astype(BF16), wd_ref[...])

    @pl.when(j == pl.num_programs(1) - 1)
    def _():
        o_ref[...] = x_ref[...] + _rms(o_ref[...], gpost_ref[...])


def _mlp(x, gpre, wu, wd, gpost, l, tm, tf):
    n, d = x.shape
    f = wu.shape[-1]
    return pl.pallas_call(
        _mlp_kernel,
        grid=(n // tm, f // tf),
        in_specs=[pl.BlockSpec((tm, d), lambda i, j: (i, 0)),
                  pl.BlockSpec((None, 1, d), lambda i, j: (l, 0, 0)),
                  pl.BlockSpec((None, d, tf), lambda i, j: (l, 0, j)),
                  pl.BlockSpec((None, tf, d), lambda i, j: (l, j, 0)),
                  pl.BlockSpec((None, 1, d), lambda i, j: (l, 0, 0))],
        out_specs=pl.BlockSpec((tm, d), lambda i, j: (i, 0)),
        out_shape=jax.ShapeDtypeStruct((n, d), F32),
        scratch_shapes=[pltpu.VMEM((tm, d), BF16)],
        compiler_params=_cp("parallel", "arbitrary"),
        name="mlp",
    )(x, gpre, wu, wd, gpost)


def _row_tile(n, target):
    t = min(n, target)
    while n % t:
        t //= 2
    return t


def kernel(x_prompt, x_sample, cache_k, cache_v, cache_logf, page_table, state_mlstm_C, state_mlstm_n, state_mlstm_m, state_ssd, state_conv, w_in, m_b_i, m_b_f, m_norm, f_b_f, s_conv_w, s_conv_b, s_dt_bias, s_A_log, s_D, s_norm, w_branch, w_out, ln_mix_pre, ln_mix_post, ln_mlp_pre, ln_mlp_post, w_up, w_down):
    bp, tp, d = x_prompt.shape
    bs, ts, _ = x_sample.shape
    depth = w_in.shape[0]
    mix = m_norm.shape[-1]
    mh, fh, sh = m_b_i.shape[-1], f_b_f.shape[-1], s_A_log.shape[-1]
    scd = s_conv_w.shape[-1]
    kconv = s_conv_w.shape[1]
    s_hd, ns = state_ssd.shape[-2], state_ssd.shape[-1]
    lay = _Layout(d, mix, mh, fh, sh, scd)
    assert w_in.shape[-1] == lay.in_width
    assert tp % CHUNK == 0 and ts <= SUBLANES
    n_pool, page = cache_k.shape[1], cache_k.shape[2]
    dk, dv, fhd = lay.dk, lay.dv, lay.fhd
    tsp = SUBLANES

    w_in_p = _pack_w_in(w_in, lay, _row_tile(d, 128))
    wb = w_branch.astype(BF16)
    wo = w_out.astype(BF16)
    wu = w_up.astype(BF16)
    wd = w_down.astype(BF16)
    assert lay.small_order == ["ff", "mi", "mf", "sdt"]
    bias_small = jnp.concatenate([f_b_f, m_b_i, m_b_f, s_dt_bias,
                                  jnp.zeros((depth, LANES - lay.small_used), F32)], axis=-1)
    bias_row = bias_small[:, None, :]
    bias_col = bias_small[:, :, None]
    alog = jnp.zeros((depth, LANES), F32).at[:, lay.lane["sdt"]:lay.lane["sdt"] + sh].set(s_A_log)
    alog_row = alog[:, None, :]
    alog_col = alog[:, :, None]
    d_exp = jnp.repeat(s_D, s_hd, axis=-1)[:, None, :]
    r3 = lambda a: a[:, None, :]
    m_norm3, s_norm3, conv_b3 = r3(m_norm), r3(s_norm), r3(s_conv_b)
    g_mix_pre, g_mix_post, g_mlp_pre, g_mlp_post = r3(ln_mix_pre), r3(ln_mix_post), r3(ln_mlp_pre), r3(ln_mlp_post)

    lf_flat = cache_logf.reshape(depth * n_pool, page * fh)
    rt = _lfsuffix(lf_flat, fh, _row_tile(depth * n_pool, 512))[:, None, :]
    npg = _row_tile(page_table.shape[1], 16)

    np_, ns_ = bp * tp, bs * ts
    xp = x_prompt.reshape(np_, d)
    xs = x_sample.reshape(ns_, d)
    tm_p = _row_tile(np_, 1024)
    tm_s = _row_tile(ns_, 1024)
    tn_in = lay.width // 11 if lay.width % (11 * LANES) == 0 else LANES
    tq = _row_tile(tp, 512)

    zero_c = jnp.zeros((1, bp, mh, dv, dk), F32)
    zero_n = jnp.zeros((1, bp, mh, 1, dk), F32)
    zero_m = jnp.zeros((1, bp, mh, 1, 1), F32)
    zero_h = jnp.zeros((1, bp, sh * s_hd, ns), F32)
    c0_s = state_mlstm_C
    n0_s = state_mlstm_n[:, :, :, None, :]
    m0_s = state_mlstm_m[:, :, :, None, None]
    h0_s = state_ssd.reshape(depth, bs, sh * s_hd, ns)
    zero_cv = jnp.zeros((bp, SUBLANES, scd), F32)

    outs_p = [[] for _ in range(8)]
    outs_s = [[] for _ in range(8)]
    for l in range(depth):
        z = _inproj(xp, g_mix_pre, w_in_p, l, tm_p, tn_in)
        z3 = z.reshape(bp, tp, lay.width)
        hm, c_p, n_p, m_p, lf_p, crow = _mlstm(z3, lay, CHUNK, bias_row, bias_col, m_norm3, l,
                                               zero_c, zero_n, zero_m, 0)
        hf = _flash(z3, crow, lay, tq)
        ysd, h_p = _ssd(z3, lay, CHUNK, bias_row, bias_col, alog_row, alog_col, s_conv_w, conv_b3, d_exp, s_norm3,
                        l, zero_h, 0, zero_cv, ns, s_hd)
        mixed = _merge(hm.reshape(np_, mix), hf.reshape(np_, mix), ysd.reshape(np_, mix), wb, z, lay, l, tm_p, 512)
        xp = _outproj(mixed, wo, g_mix_post, xp, l, _row_tile(np_, 512))
        xp = _mlp(xp, g_mlp_pre, wu, wd, g_mlp_post, l, tm_p, 512)
        fk_o, fv_o, u_o = lay.dst["fk"], lay.dst["fv"], lay.dst["sxbc"]
        for lst, a in zip(outs_p, (z3[:, :, fk_o:fk_o + mix].reshape(bp, tp, fh, fhd),
                                   z3[:, :, fv_o:fv_o + mix].reshape(bp, tp, fh, fhd),
                                   lf_p, c_p, n_p.reshape(bp, mh, dk), m_p.reshape(bp, mh),
                                   h_p.reshape(bp, sh, s_hd, ns), z3[:, tp - (kconv - 1):, u_o:u_o + scd])):
            lst.append(a)

        zs = _inproj(xs, g_mix_pre, w_in_p, l, tm_s, tn_in)
        zs3 = zs.reshape(bs, ts, lay.width)
        zs8 = jnp.pad(zs3, ((0, 0), (0, tsp - ts), (0, 0)))
        hm_s, c_s, n_s, m_s, lf_s, crow_s = _mlstm(zs8, lay, ts, bias_row, bias_col, m_norm3, l,
                                                   c0_s, n0_s, m0_s, l)
        fnew = jnp.transpose(crow_s[:, :, :ts], (0, 2, 1)).reshape(bs, 1, ts * fh)
        fnew = jnp.pad(fnew, ((0, 0), (0, 0), (0, LANES - ts * fh)))
        z5 = zs8.reshape(bs, tsp, lay.width // LANES, LANES)
        hf_s = _decode(z5, fnew, cache_k, cache_v, rt, page_table, lay, l, ts, npg)
        cv0 = jnp.pad(state_conv[l], ((0, 0), (SUBLANES - (kconv - 1), 0), (0, 0)))
        ys_s, h_s = _ssd(zs8, lay, ts, bias_row, bias_col, alog_row, alog_col, s_conv_w, conv_b3, d_exp, s_norm3,
                         l, h0_s, l, cv0, ns, s_hd)
        mixed_s = _merge(hm_s[:, :ts].reshape(ns_, mix), hf_s.reshape(ns_, mix).astype(BF16),
                         ys_s[:, :ts].reshape(ns_, mix), wb, zs, lay, l, tm_s, 512)
        xs = _outproj(mixed_s, wo, g_mix_post, xs, l, tm_s)
        xs = _mlp(xs, g_mlp_pre, wu, wd, g_mlp_post, l, tm_s, 512)
        conv_full = jnp.concatenate([state_conv[l], zs3[:, :, u_o:u_o + scd]], axis=1)
        for lst, a in zip(outs_s, (zs3[:, :, fk_o:fk_o + mix].reshape(bs, ts, fh, fhd),
                                   zs3[:, :, fv_o:fv_o + mix].reshape(bs, ts, fh, fhd),
                                   lf_s[:, :ts], c_s, n_s.reshape(bs, mh, dk), m_s.reshape(bs, mh),
                                   h_s.reshape(bs, sh, s_hd, ns), conv_full[:, -(kconv - 1):])):
            lst.append(a)

    k_p, v_p, lf_pp, c_pp, n_pp, m_pp, ssd_p, conv_p = [jnp.stack(a) for a in outs_p]
    k_s, v_s, lf_ss, c_ss, n_ss, m_ss, ssd_s, conv_s = [jnp.stack(a) for a in outs_s]
    return (xp.reshape(bp, tp, d), xs.reshape(bs, ts, d), k_p, v_p, lf_pp, k_s, v_s, lf_ss,
            c_pp, n_pp, m_pp, c_ss, n_ss, m_ss, ssd_p, ssd_s, conv_p, conv_s)
```

```python
import functools

import jax
import jax.numpy as jnp
from jax import lax
from jax.experimental import pallas as pl
from jax.experimental.pallas import tpu as pltpu

F32 = jnp.float32
BF16 = jnp.bfloat16
HI = lax.Precision.HIGHEST
EPS = 1e-6
NEG_INF = float("-inf")
LOG2E = 1.4426950408889634
LANES = 128
SUBLANES = 8
CHUNK = 128
VMEM_LIMIT = 56 * 1024 * 1024


def _cp(*sem, vmem=VMEM_LIMIT):
    return pltpu.CompilerParams(dimension_semantics=sem, vmem_limit_bytes=vmem)


def _nt(a, b):
    return lax.dot_general(a, b, (((1,), (1,)), ((), ())), preferred_element_type=F32)


def _mm(a, b):
    return jnp.dot(a, b, preferred_element_type=F32)


def _mm_exact(a, b):
    return jnp.dot(a, b, preferred_element_type=F32, precision=HI)


def _softplus(x):
    return jnp.maximum(x, 0.0) + jnp.log1p(jnp.exp(-jnp.abs(x)))


def _log_sigmoid(x):
    return -_softplus(-x)


def _sigmoid(x):
    return 1.0 / (1.0 + jnp.exp(-x))


def _silu(x):
    return x * _sigmoid(x)


def _rms(x, g):
    return x * lax.rsqrt(jnp.mean(x * x, axis=-1, keepdims=True) + EPS) * g


def _pad_rows(a, rows):
    if a.shape[0] == rows:
        return a
    return jnp.concatenate([a, jnp.zeros((rows - a.shape[0], a.shape[1]), a.dtype)], axis=0)


def _imod(x, n):
    assert n & (n - 1) == 0
    return x & (n - 1)


def _idiv(x, n):
    assert n & (n - 1) == 0
    return x >> (n.bit_length() - 1)


def _pick_col(a, sel):
    return jnp.sum(jnp.where(sel, a, 0.0), axis=1, keepdims=True)


class _Layout:
    def __init__(self, d_model, mix, mh, fh, sh, scd):
        self.d, self.mix, self.mh, self.fh, self.sh, self.scd = d_model, mix, mh, fh, sh, scd
        self.dk = mix // mh // 2
        self.dv = mix // mh
        self.fhd = mix // fh
        src = [("mq", mh * self.dk), ("mk", mh * self.dk), ("mv", mix), ("mi", mh), ("mf", mh), ("mo", mix),
               ("fq", mix), ("fk", mix), ("fv", mix), ("ff", fh), ("sz", mix), ("sxbc", scd), ("sdt", sh),
               ("g", 3 * d_model)]
        self.src = {}
        off = 0
        for name, w in src:
            self.src[name] = (off, w)
            off += w
        self.in_width = off
        order = ["mq", "mk", "mv", "sxbc", "mo", "fq", "fk", "fv", "sz", "g"]
        self.dst = {}
        off = 0
        for name in order:
            self.dst[name] = off
            off += self.src[name][1]
        self.small = off
        self.small_order = ["ff", "mi", "mf", "sdt"]
        self.lane = {}
        l = 0
        for name in self.small_order:
            self.lane[name] = l
            l += self.src[name][1]
        assert l <= LANES
        self.small_used = l
        self.order = order
        self.width = off + LANES


def _pack_kernel(tab_ref, w_ref, small_ref, o_ref, *, n_big):
    i = pl.program_id(1)

    @pl.when(i < n_big)
    def _():
        o_ref[...] = w_ref[0].T.astype(BF16)

    @pl.when(i == n_big)
    def _():
        o_ref[...] = small_ref[...].T.astype(BF16)


def _pack_w_in(w_in, lay):
    depth, d, _ = w_in.shape
    wt = jnp.swapaxes(w_in, 1, 2)
    starts = []
    for name in lay.order:
        src, width = lay.src[name]
        assert src % SUBLANES == 0 and width % LANES == 0
        starts += [src + kblk * LANES for kblk in range(width // LANES)]
    n_big = len(starts)
    table = jnp.asarray([s // SUBLANES for s in starts] + [0], jnp.int32)
    small = jnp.concatenate([wt[:, lay.src[n][0]:lay.src[n][0] + lay.src[n][1], :] for n in lay.small_order]
                            + [jnp.zeros((depth, LANES - lay.small_used, d), F32)], axis=1)
    grid_spec = pltpu.PrefetchScalarGridSpec(
        num_scalar_prefetch=1,
        grid=(depth, n_big + 1),
        in_specs=[pl.BlockSpec((pl.Element(1), pl.Element(LANES), pl.Element(d)),
                               lambda l, i, tab: (l, tab[i] * SUBLANES, 0)),
                  pl.BlockSpec((None, LANES, d), lambda l, i, tab: (l, 0, 0))],
        out_specs=pl.BlockSpec((None, d, LANES), lambda l, i, tab: (l, 0, i)),
    )
    return pl.pallas_call(
        functools.partial(_pack_kernel, n_big=n_big),
        grid_spec=grid_spec,
        out_shape=jax.ShapeDtypeStruct((depth, d, lay.width), BF16),
        compiler_params=_cp("parallel", "arbitrary"),
        name="pack_w_in",
    )(table, wt, small)


def _inproj_kernel(x_ref, g_ref, w_ref, o_ref, h_ref):
    @pl.when(pl.program_id(1) == 0)
    def _():
        h_ref[...] = _rms(x_ref[...], g_ref[...]).astype(BF16)

    o_ref[...] = _mm(h_ref[...], w_ref[...])


def _inproj(x, g, w, l, tm, tn):
    n, d = x.shape
    width = w.shape[-1]
    return pl.pallas_call(
        _inproj_kernel,
        grid=(n // tm, width // tn),
        in_specs=[pl.BlockSpec((tm, d), lambda i, j: (i, 0)),
                  pl.BlockSpec((None, 1, d), lambda i, j: (l, 0, 0)),
                  pl.BlockSpec((None, d, tn), lambda i, j: (l, 0, j))],
        out_specs=pl.BlockSpec((tm, tn), lambda i, j: (i, j)),
        out_shape=jax.ShapeDtypeStruct((n, width), F32),
        scratch_shapes=[pltpu.VMEM((tm, d), BF16)],
        compiler_params=_cp("parallel", "arbitrary"),
        name="inproj",
    )(x, g, w)


def _mlstm_kernel(q_ref, k_ref, v_ref, og_ref, s_ref, brow_ref, bcol_ref, mn_ref, c0_ref, n0_ref, m0_ref,
                  hm_ref, cout_ref, nout_ref, mout_ref, lf_ref, crow_ref, ct_ref, n_ref, m_ref, fcarry_ref, *,
                  L, lr, nh, dk, dv, lane_i, lane_f, lane_ff, nfh, kscale):
    c = pl.program_id(1)
    nc = pl.num_programs(1)

    @pl.when(c == 0)
    def _():
        for h in range(nh):
            ct_ref[h] = c0_ref[0, h].T
        n_ref[...] = n0_ref[0]
        m_ref[...] = m0_ref[0]
        fcarry_ref[...] = jnp.zeros_like(fcarry_ref)

    gates = _pad_rows(s_ref[0], L)
    row = lax.broadcasted_iota(jnp.int32, (L, 1), 0)
    col = lax.broadcasted_iota(jnp.int32, (1, L), 1)
    lane = lax.broadcasted_iota(jnp.int32, (1, LANES), 1)
    valid_c = row < lr
    valid_r = col < lr
    causal = col <= row
    gb = gates + brow_ref[...]
    gt = gates.T + bcol_ref[...]
    lf = jnp.where(valid_c, _log_sigmoid(gb), 0.0)
    lft = jnp.where(valid_r, _log_sigmoid(gt), 0.0)
    bc_all = _mm_exact(causal.astype(F32), lf)
    br_all = _mm_exact(lft, (row <= col).astype(F32))
    rows_out = hm_ref.shape[1]

    lf_ref[0] = lf[:rows_out, lane_ff:lane_ff + nfh]
    csum = br_all[lane_ff:lane_ff + nfh] + fcarry_ref[...]
    crow_ref[0] = csum
    fcarry_ref[...] = csum[:, L - 1:L]

    for h in range(nh):
        q = _pad_rows(q_ref[0, :, h * dk:(h + 1) * dk], L)
        k = _pad_rows(k_ref[0, :, h * dk:(h + 1) * dk], L) * kscale
        v = _pad_rows(v_ref[0, :, h * dv:(h + 1) * dv], L)
        i_c = jnp.where(valid_c, _pick_col(gb, lane == lane_i + h), NEG_INF)
        i_r = jnp.where(valid_r, gt[lane_i + h:lane_i + h + 1, :], NEG_INF)
        b_c = _pick_col(bc_all, lane == lane_f + h)
        b_r = br_all[lane_f + h:lane_f + h + 1, :]

        m_prev = m_ref[h]
        log_d = jnp.where(causal, b_c + (i_r - b_r), NEG_INF)
        inter = b_c + m_prev
        m_t = jnp.maximum(inter, jnp.max(log_d, axis=1, keepdims=True))
        w_inter = jnp.exp(inter - m_t)
        qb = q.astype(BF16)
        kb = k.astype(BF16)
        vb = v.astype(BF16)
        s = _nt(qb, kb) * jnp.exp(log_d - m_t)
        ct = ct_ref[h]
        nvec = n_ref[h]
        num = _mm(s.astype(BF16), vb) + w_inter * _mm(qb, ct.astype(BF16))
        nq = jnp.sum(qb.astype(F32) * nvec.astype(BF16).astype(F32), axis=1, keepdims=True)
        den = jnp.sum(s, axis=1, keepdims=True) + w_inter * nq
        hh = num / jnp.maximum(jnp.abs(den), jnp.exp(-m_t))

        b_last = b_c[L - 1:L, :]
        le_c = b_last - b_c + i_c
        le_r = b_last - b_r + i_r
        m_new = jnp.maximum(b_last + m_prev, jnp.max(le_r, axis=1, keepdims=True))
        we_c = jnp.exp(le_c - m_new)
        we_r = jnp.exp(le_r - m_new)
        ws = jnp.exp(b_last + m_prev - m_new)
        ct_ref[h] = ws * ct + _mm(k.T.astype(BF16), (we_c * v).astype(BF16))
        n_ref[h] = ws * nvec + _mm(jnp.broadcast_to(we_r, (SUBLANES, L)).astype(BF16), kb)[0:1]
        m_ref[h] = m_new

        y = _rms(hh, mn_ref[:, h * dv:(h + 1) * dv]) * _sigmoid(_pad_rows(og_ref[0, :, h * dv:(h + 1) * dv], L))
        hm_ref[0, :, h * dv:(h + 1) * dv] = y[:rows_out].astype(BF16)

    @pl.when(c == nc - 1)
    def _():
        for h in range(nh):
            cout_ref[0, h] = ct_ref[h].T
        nout_ref[0] = n_ref[...]
        mout_ref[0] = m_ref[...]


def _mlstm(z3, lay, lr, bias_row, bias_col, m_norm, l, c0, n0, m0, ls):
    bsz, t, _ = z3.shape
    L = CHUNK
    rb = min(t, L)
    nc = t // rb
    mh, dk, dv = lay.mh, lay.dk, lay.dv
    fh = lay.fh
    kern = functools.partial(_mlstm_kernel, L=L, lr=lr, nh=mh, dk=dk, dv=dv, lane_i=lay.lane["mi"],
                             lane_f=lay.lane["mf"], lane_ff=lay.lane["ff"], nfh=fh, kscale=float(dk) ** -0.5)
    qw, vw = mh * dk, mh * dv
    qo, ko, vo, oo, so = (lay.dst["mq"] // qw, lay.dst["mk"] // qw, lay.dst["mv"] // vw, lay.dst["mo"] // vw,
                          lay.small // LANES)
    return pl.pallas_call(
        kern,
        grid=(bsz, nc),
        in_specs=[pl.BlockSpec((1, rb, qw), lambda b, c: (b, c, qo)),
                  pl.BlockSpec((1, rb, qw), lambda b, c: (b, c, ko)),
                  pl.BlockSpec((1, rb, vw), lambda b, c: (b, c, vo)),
                  pl.BlockSpec((1, rb, vw), lambda b, c: (b, c, oo)),
                  pl.BlockSpec((1, rb, LANES), lambda b, c: (b, c, so)),
                  pl.BlockSpec((None, 1, LANES), lambda b, c: (l, 0, 0)),
                  pl.BlockSpec((None, LANES, 1), lambda b, c: (l, 0, 0)),
                  pl.BlockSpec((None, 1, vw), lambda b, c: (l, 0, 0)),
                  pl.BlockSpec((None, 1, mh, dv, dk), lambda b, c: (ls, b, 0, 0, 0)),
                  pl.BlockSpec((None, 1, mh, 1, dk), lambda b, c: (ls, b, 0, 0, 0)),
                  pl.BlockSpec((None, 1, mh, 1, 1), lambda b, c: (ls, b, 0, 0, 0))],
        out_specs=[pl.BlockSpec((1, rb, vw), lambda b, c: (b, c, 0)),
                   pl.BlockSpec((1, mh, dv, dk), lambda b, c: (b, 0, 0, 0)),
                   pl.BlockSpec((1, mh, 1, dk), lambda b, c: (b, 0, 0, 0)),
                   pl.BlockSpec((1, mh, 1, 1), lambda b, c: (b, 0, 0, 0)),
                   pl.BlockSpec((1, rb, fh), lambda b, c: (b, c, 0)),
                   pl.BlockSpec((1, fh, L), lambda b, c: (b, 0, c))],
        out_shape=[jax.ShapeDtypeStruct((bsz, t, vw), BF16),
                   jax.ShapeDtypeStruct((bsz, mh, dv, dk), F32),
                   jax.ShapeDtypeStruct((bsz, mh, 1, dk), F32),
                   jax.ShapeDtypeStruct((bsz, mh, 1, 1), F32),
                   jax.ShapeDtypeStruct((bsz, t, fh), F32),
                   jax.ShapeDtypeStruct((bsz, fh, nc * L), F32)],
        scratch_shapes=[pltpu.VMEM((mh, dk, dv), F32), pltpu.VMEM((mh, 1, dk), F32), pltpu.VMEM((mh, 1, 1), F32),
                        pltpu.VMEM((fh, 1), F32)],
        compiler_params=_cp("parallel", "arbitrary"),
        name="mlstm",
    )(z3, z3, z3, z3, z3, bias_row, bias_col, m_norm, c0, n0, m0)


def _flash_kernel(q_ref, k_ref, v_ref, c_ref, o_ref, kb_ref, vb_ref, *, tq, scale):
    qi = pl.program_id(2)

    @pl.when(qi == 0)
    def _():
        kb_ref[...] = k_ref[0].astype(BF16)
        vb_ref[...] = v_ref[0].astype(BF16)

    q = (q_ref[0] * (scale * LOG2E)).astype(BF16)
    hd = q.shape[1]
    causal = lax.broadcasted_iota(jnp.int32, (1, tq), 1) <= lax.broadcasted_iota(jnp.int32, (tq, 1), 0)

    def block(j, carry, diagonal):
        m_prev, l_prev, acc = carry
        off = pl.multiple_of(j * tq, tq)
        s = _nt(q, kb_ref[pl.ds(off, tq), :]) - c_ref[0, 0, j] * LOG2E
        if diagonal:
            s = jnp.where(causal, s, NEG_INF)
        m_new = jnp.maximum(m_prev, jnp.max(s, axis=1, keepdims=True))
        alpha = jnp.exp2(m_prev - m_new)
        p = jnp.exp2(s - m_new)
        l_new = alpha * l_prev + jnp.sum(p, axis=1, keepdims=True)
        acc = alpha * acc + _mm(p.astype(BF16), vb_ref[pl.ds(off, tq), :])
        return m_new, l_new, acc

    init = (jnp.full((tq, 1), NEG_INF, F32), jnp.zeros((tq, 1), F32), jnp.zeros((tq, hd), F32))
    carry = lax.fori_loop(0, qi, lambda j, c: block(j, c, False), init)
    _, l_fin, acc = block(qi, carry, True)
    o_ref[0] = (acc / l_fin).astype(BF16)


def _flash(z3, crow, lay, tq):
    bsz, t, _ = z3.shape
    fh, hd = lay.fh, lay.fhd
    qo, ko, vo = lay.dst["fq"] // hd, lay.dst["fk"] // hd, lay.dst["fv"] // hd
    nq = t // tq
    c5 = crow.reshape(bsz, fh, nq, 1, tq)
    return pl.pallas_call(
        functools.partial(_flash_kernel, tq=tq, scale=float(hd) ** -0.5),
        grid=(bsz, fh, nq),
        in_specs=[pl.BlockSpec((1, tq, hd), lambda b, h, qi: (b, qi, qo + h)),
                  pl.BlockSpec((1, t, hd), lambda b, h, qi: (b, 0, ko + h)),
                  pl.BlockSpec((1, t, hd), lambda b, h, qi: (b, 0, vo + h)),
                  pl.BlockSpec((1, 1, nq, 1, tq), lambda b, h, qi: (b, h, 0, 0, 0))],
        out_specs=pl.BlockSpec((1, tq, hd), lambda b, h, qi: (b, qi, h)),
        out_shape=jax.ShapeDtypeStruct((bsz, t, fh * hd), BF16),
        scratch_shapes=[pltpu.VMEM((t, hd), BF16), pltpu.VMEM((t, hd), BF16)],
        compiler_params=_cp("parallel", "parallel", "arbitrary"),
        name="fox_flash",
    )(z3, z3, z3, c5)


def _lfsuffix_kernel(x_ref, o_ref, mat_ref, *, nh):
    w = x_ref.shape[1]

    @pl.when(pl.program_id(0) == 0)
    def _():
        ri = lax.broadcasted_iota(jnp.int32, (w, 1), 0)
        ci = lax.broadcasted_iota(jnp.int32, (1, w), 1)
        same = _imod(ri, nh) == _imod(ci, nh)
        mat_ref[:, 0:w] = jnp.where(same & (_idiv(ri, nh) > _idiv(ci, nh)), 1.0, 0.0).astype(BF16)
        mat_ref[:, w:2 * w] = jnp.where(same, 1.0, 0.0).astype(BF16)

    x = x_ref[...]
    x1 = x.astype(BF16)
    r1 = x - x1.astype(F32)
    x2 = r1.astype(BF16)
    x3 = (r1 - x2.astype(F32)).astype(BF16)
    mat = mat_ref[...]
    o_ref[...] = _mm(x1, mat) + (_mm(x2, mat) + _mm(x3, mat))


def _lfsuffix(lf_flat, nh, tm):
    n, w = lf_flat.shape
    return pl.pallas_call(
        functools.partial(_lfsuffix_kernel, nh=nh),
        grid=(n // tm,),
        in_specs=[pl.BlockSpec((tm, w), lambda i: (i, 0))],
        out_specs=pl.BlockSpec((tm, 2 * w), lambda i: (i, 0)),
        out_shape=jax.ShapeDtypeStruct((n, 2 * w), F32),
        scratch_shapes=[pltpu.VMEM((w, 2 * w), BF16)],
        compiler_params=_cp("arbitrary"),
        name="lf_suffix",
    )(lf_flat)


def _decode_kernel(pt_ref, q_ref, kn_ref, vn_ref, fn_ref, *refs, scale, nq, nh, npg):
    k_refs, v_refs, rt_refs = refs[0:npg], refs[npg:2 * npg], refs[2 * npg:3 * npg]
    o_ref, m_ref, l_ref, acc_ref, tail_ref = refs[3 * npg:]
    j = pl.program_id(1)
    nj = pl.num_programs(1)
    rows = nq * nh
    hd = q_ref.shape[-1]
    w = k_refs[0].shape[0] * k_refs[0].shape[1]

    @pl.when(j == 0)
    def _():
        m_ref[...] = jnp.full_like(m_ref, NEG_INF)
        l_ref[...] = jnp.zeros_like(l_ref)
        acc_ref[...] = jnp.zeros_like(acc_ref)
        tail_ref[...] = jnp.zeros_like(tail_ref)

    qs = q_ref[0].reshape(rows, hd).astype(BF16)
    row_h = _imod(lax.broadcasted_iota(jnp.int32, (rows, 1), 0), nh)

    def update(scores, values):
        m_prev = m_ref[...]
        smax = scores[0]
        for s in scores[1:]:
            smax = jnp.maximum(smax, s)
        m_new = jnp.maximum(m_prev, jnp.max(smax, axis=1, keepdims=True))
        alpha = jnp.exp(m_prev - m_new)
        psum = None
        pv = None
        for s, vb in zip(scores, values):
            p = jnp.exp(s - m_new)
            psum = p if psum is None else psum + p
            d = _mm(p.astype(BF16), vb)
            pv = d if pv is None else pv + d
        l_ref[...] = alpha * l_ref[...] + jnp.sum(psum, axis=1, keepdims=True)
        acc_ref[...] = alpha * acc_ref[...] + pv
        m_ref[...] = m_new

    same_head = _imod(lax.broadcasted_iota(jnp.int32, (1, w), 1), nh) == row_h
    tail = tail_ref[...]
    scores, values = [], []
    for i in range(npg):
        kf = k_refs[i][...].reshape(w, hd).astype(BF16)
        rt = rt_refs[i][0]
        s = _nt(qs, kf) * scale + (rt[:, 0:w] + tail)
        scores.append(jnp.where(same_head, s, NEG_INF))
        values.append(v_refs[i][...].reshape(w, hd).astype(BF16))
        tail = tail + rt[:, w:2 * w]
    tail_ref[...] = tail
    update(scores, values)

    @pl.when(j == nj - 1)
    def _():
        kn = _pad_rows(kn_ref[0].reshape(rows, hd), LANES).astype(BF16)
        vn = _pad_rows(vn_ref[0].reshape(rows, hd), LANES).astype(BF16)
        lane = lax.broadcasted_iota(jnp.int32, (1, LANES), 1)
        row_q = _idiv(lax.broadcasted_iota(jnp.int32, (rows, 1), 0), nh)
        ok = (_imod(lane, nh) == row_h) & (_idiv(lane, nh) <= row_q) & (lane < rows)
        s_new = _nt(qs, kn) * scale - fn_ref[0]
        update([jnp.where(ok, s_new, NEG_INF)], [vn])
        o_ref[0] = (acc_ref[...] / l_ref[...]).reshape(nq, nh, hd)


def _decode(z5, fnew, cache_k, cache_v, rt, page_table, lay, l, nq, npg):
    bsz = z5.shape[0]
    fh, hd = lay.fh, lay.fhd
    n_pool, page = cache_k.shape[1], cache_k.shape[2]
    n_pages = page_table.shape[1]
    w = page * fh
    qo, ko, vo = lay.dst["fq"] // (fh * hd), lay.dst["fk"] // (fh * hd), lay.dst["fv"] // (fh * hd)

    def kv_spec(i):
        return pl.BlockSpec((None, None, page, fh, hd),
                            lambda b, j, pt: (l, pt[b, n_pages - 1 - (j * npg + i)], 0, 0, 0))

    def rt_spec(i):
        return pl.BlockSpec((1, 1, 2 * w), lambda b, j, pt: (l * n_pool + pt[b, n_pages - 1 - (j * npg + i)], 0, 0))

    grid_spec = pltpu.PrefetchScalarGridSpec(
        num_scalar_prefetch=1,
        grid=(bsz, n_pages // npg),
        in_specs=[pl.BlockSpec((1, nq, fh, hd), lambda b, j, pt: (b, 0, qo, 0)),
                  pl.BlockSpec((1, nq, fh, hd), lambda b, j, pt: (b, 0, ko, 0)),
                  pl.BlockSpec((1, nq, fh, hd), lambda b, j, pt: (b, 0, vo, 0)),
                  pl.BlockSpec((1, 1, LANES), lambda b, j, pt: (b, 0, 0))]
                 + [kv_spec(i) for i in range(npg)] + [kv_spec(i) for i in range(npg)]
                 + [rt_spec(i) for i in range(npg)],
        out_specs=pl.BlockSpec((1, nq, fh, hd), lambda b, j, pt: (b, 0, 0, 0)),
        scratch_shapes=[pltpu.VMEM((nq * fh, 1), F32), pltpu.VMEM((nq * fh, 1), F32),
                        pltpu.VMEM((nq * fh, hd), F32), pltpu.VMEM((1, w), F32)],
    )
    return pl.pallas_call(
        functools.partial(_decode_kernel, scale=float(hd) ** -0.5, nq=nq, nh=fh, npg=npg),
        grid_spec=grid_spec,
        out_shape=jax.ShapeDtypeStruct((bsz, nq, fh, hd), F32),
        compiler_params=_cp("parallel", "arbitrary"),
        name="fox_decode",
    )(page_table, z5, z5, z5, fnew, *([cache_k] * npg), *([cache_v] * npg), *([rt] * npg))


def _ssd_kernel(u_ref, sz_ref, s_ref, brow_ref, bcol_ref, alrow_ref, alcol_ref, cw_ref, cb_ref, dexp_ref, sn_ref,
                h0_ref, cv0_ref, y_ref, hout_ref, xs_ref, ht_ref, *, L, lr, ng, hpg, hd, ns, lane_dt, kconv):
    c = pl.program_id(1)
    nc = pl.num_programs(1)
    mix = ng * hpg * hd
    gw = hpg * hd

    @pl.when(c == 0)
    def _():
        xs_ref[0:SUBLANES, :] = cv0_ref[0]
        for g in range(ng):
            ht_ref[g] = h0_ref[0, g * gw:(g + 1) * gw, :].T

    @pl.when(c > 0)
    def _():
        xs_ref[0:SUBLANES, :] = xs_ref[L:L + SUBLANES, :]

    xs_ref[SUBLANES:SUBLANES + L, :] = _pad_rows(u_ref[0], L)
    conv = cb_ref[...]
    for jj in range(kconv):
        off = SUBLANES - (kconv - 1) + jj
        conv = conv + cw_ref[jj:jj + 1, :] * xs_ref[pl.ds(off, L), :]
    xbc = _silu(conv)
    x = xbc[:, :mix]

    gates = _pad_rows(s_ref[0], L)
    row = lax.broadcasted_iota(jnp.int32, (L, 1), 0)
    col = lax.broadcasted_iota(jnp.int32, (1, L), 1)
    lane = lax.broadcasted_iota(jnp.int32, (1, LANES), 1)
    causal = col <= row
    dt_c = jnp.where(row < lr, _softplus(gates + brow_ref[...]), 0.0)
    dt_r = jnp.where(col < lr, _softplus(gates.T + bcol_ref[...]), 0.0)
    cum_c = _mm_exact(causal.astype(F32), dt_c * (-jnp.exp(alrow_ref[...])))
    cum_r = _mm_exact(dt_r * (-jnp.exp(alcol_ref[...])), (row <= col).astype(F32))
    lane_grp = _idiv(lax.broadcasted_iota(jnp.int32, (1, gw), 1), hd)

    ys = []
    for g in range(ng):
        bg = xbc[:, mix + g * ns:mix + (g + 1) * ns]
        cg = xbc[:, mix + (ng + g) * ns:mix + (ng + g + 1) * ns]
        bb = bg.astype(BF16)
        cb = cg.astype(BF16)
        cbm = _nt(cb, bb)
        xg = x[:, g * gw:(g + 1) * gw]
        ht = ht_ref[g]
        y = jnp.zeros((L, gw), F32)
        ecum = jnp.zeros((L, gw), F32)
        wend = jnp.zeros((L, gw), F32)
        dec = jnp.zeros((1, gw), F32)
        for r in range(hpg):
            idx = lane_dt + g * hpg + r
            cc = _pick_col(cum_c, lane == idx)
            dtc = _pick_col(dt_c, lane == idx)
            cr = cum_r[idx:idx + 1, :]
            dtr = dt_r[idx:idx + 1, :]
            wmat = cbm * jnp.exp(jnp.where(causal, cc - cr, NEG_INF)) * dtr
            sel = lane_grp == r
            y = y + _mm(wmat.astype(BF16), jnp.where(sel, xg, 0.0).astype(BF16))
            ce = cc[L - 1:L, :]
            ecum = jnp.where(sel, jnp.exp(cc), ecum)
            wend = jnp.where(sel, jnp.exp(ce - cc) * dtc, wend)
            dec = jnp.where(sel, jnp.exp(ce), dec)
        y = y + ecum * _mm(cb, ht.astype(BF16))
        ht_ref[g] = dec * ht + _mm(bg.T.astype(BF16), (wend * xg).astype(BF16))
        ys.append(y + dexp_ref[:, g * gw:(g + 1) * gw] * xg)
    yall = jnp.concatenate(ys, axis=1) * _silu(_pad_rows(sz_ref[0], L))
    y_ref[0] = _rms(yall, sn_ref[...])[:y_ref.shape[1]].astype(BF16)

    @pl.when(c == nc - 1)
    def _():
        for g in range(ng):
            hout_ref[0, g * gw:(g + 1) * gw, :] = ht_ref[g].T


def _ssd(z3, lay, lr, bias_row, bias_col, alog_row, alog_col, conv_w, conv_b, d_exp, s_norm, l, h0, ls, cv0, ns, hd):
    bsz, t, _ = z3.shape
    L = CHUNK
    rb = min(t, L)
    nc = t // rb
    mix, scd, sh = lay.mix, lay.scd, lay.sh
    ng = (scd - mix) // (2 * ns)
    hpg = sh // ng
    kconv = conv_w.shape[1]
    kern = functools.partial(_ssd_kernel, L=L, lr=lr, ng=ng, hpg=hpg, hd=hd, ns=ns, lane_dt=lay.lane["sdt"],
                             kconv=kconv)
    uo, zo, so = lay.dst["sxbc"] // scd, lay.dst["sz"] // mix, lay.small // LANES
    return pl.pallas_call(
        kern,
        grid=(bsz, nc),
        in_specs=[pl.BlockSpec((1, rb, scd), lambda b, c: (b, c, uo)),
                  pl.BlockSpec((1, rb, mix), lambda b, c: (b, c, zo)),
                  pl.BlockSpec((1, rb, LANES), lambda b, c: (b, c, so)),
                  pl.BlockSpec((None, 1, LANES), lambda b, c: (l, 0, 0)),
                  pl.BlockSpec((None, LANES, 1), lambda b, c: (l, 0, 0)),
                  pl.BlockSpec((None, 1, LANES), lambda b, c: (l, 0, 0)),
                  pl.BlockSpec((None, LANES, 1), lambda b, c: (l, 0, 0)),
                  pl.BlockSpec((None, kconv, scd), lambda b, c: (l, 0, 0)),
                  pl.BlockSpec((None, 1, scd), lambda b, c: (l, 0, 0)),
                  pl.BlockSpec((None, 1, mix), lambda b, c: (l, 0, 0)),
                  pl.BlockSpec((None, 1, mix), lambda b, c: (l, 0, 0)),
                  pl.BlockSpec((None, 1, sh * hd, ns), lambda b, c: (ls, b, 0, 0)),
                  pl.BlockSpec((1, SUBLANES, scd), lambda b, c: (b, 0, 0))],
        out_specs=[pl.BlockSpec((1, rb, mix), lambda b, c: (b, c, 0)),
                   pl.BlockSpec((1, sh * hd, ns), lambda b, c: (b, 0, 0))],
        out_shape=[jax.ShapeDtypeStruct((bsz, t, mix), BF16),
                   jax.ShapeDtypeStruct((bsz, sh * hd, ns), F32)],
        scratch_shapes=[pltpu.VMEM((L + 2 * SUBLANES, scd), F32), pltpu.VMEM((ng, ns, hpg * hd), F32)],
        compiler_params=_cp("parallel", "arbitrary"),
        name="ssd",
    )(z3, z3, z3, bias_row, bias_col, alog_row, alog_col, conv_w, conv_b, d_exp, s_norm, h0, cv0)


def _merge_kernel(hm_ref, hf_ref, ys_ref, wb_ref, g0_ref, g1_ref, g2_ref, o_ref):
    acc = _sigmoid(g0_ref[...]) * _mm(hm_ref[...], wb_ref[0])
    acc = acc + _sigmoid(g1_ref[...]) * _mm(hf_ref[...], wb_ref[1])
    acc = acc + _sigmoid(g2_ref[...]) * _mm(ys_ref[...], wb_ref[2])
    o_ref[...] = acc.astype(BF16)


def _merge(hm, hf, ys, wb, z, lay, l, tm, tn):
    n, mix = hm.shape
    d = lay.d
    go = lay.dst["g"] // tn
    per = d // tn
    gspec = lambda k: pl.BlockSpec((tm, tn), lambda i, j: (i, go + k * per + j))
    act = pl.BlockSpec((tm, mix), lambda i, j: (i, 0))
    return pl.pallas_call(
        _merge_kernel,
        grid=(n // tm, d // tn),
        in_specs=[act, act, act, pl.BlockSpec((None, 3, mix, tn), lambda i, j: (l, 0, 0, j)),
                  gspec(0), gspec(1), gspec(2)],
        out_specs=pl.BlockSpec((tm, tn), lambda i, j: (i, j)),
        out_shape=jax.ShapeDtypeStruct((n, d), BF16),
        compiler_params=_cp("parallel", "arbitrary"),
        name="merge",
    )(hm, hf, ys, wb, z, z, z)


def _outproj_kernel(a_ref, w_ref, g_ref, x_ref, o_ref):
    o_ref[...] = x_ref[...] + _rms(_mm(a_ref[...], w_ref[...]), g_ref[...])


def _outproj(a, w, g, x, l, tm):
    n, d = x.shape
    return pl.pallas_call(
        _outproj_kernel,
        grid=(n // tm,),
        in_specs=[pl.BlockSpec((tm, d), lambda i: (i, 0)),
                  pl.BlockSpec((None, d, d), lambda i: (l, 0, 0)),
                  pl.BlockSpec((None, 1, d), lambda i: (l, 0, 0)),
                  pl.BlockSpec((tm, d), lambda i: (i, 0))],
        out_specs=pl.BlockSpec((tm, d), lambda i: (i, 0)),
        out_shape=jax.ShapeDtypeStruct((n, d), F32),
        compiler_params=_cp("parallel"),
        name="outproj",
    )(a, w, g, x)


def _mlp_kernel(x_ref, gpre_ref, wu_ref, wd_ref, gpost_ref, o_ref, h_ref):
    j = pl.program_id(1)

    @pl.when(j == 0)
    def _():
        h_ref[...] = _rms(x_ref[...], gpre_ref[...]).astype(BF16)
        o_ref[...] = jnp.zeros_like(o_ref)

    u = jnp.maximum(_mm(h_ref[...], wu_ref[...]), 0.0)
    o_ref[...] += _mm((u * u).astype(BF16), wd_ref[...])

    @pl.when(j == pl.num_programs(1) - 1)
    def _():
        o_ref[...] = x_ref[...] + _rms(o_ref[...], gpost_ref[...])


def _mlp(x, gpre, wu, wd, gpost, l, tm, tf):
    n, d = x.shape
    f = wu.shape[-1]
    return pl.pallas_call(
        _mlp_kernel,
        grid=(n // tm, f // tf),
        in_specs=[pl.BlockSpec((tm, d), lambda i, j: (i, 0)),
                  pl.BlockSpec((None, 1, d), lambda i, j: (l, 0, 0)),
                  pl.BlockSpec((None, d, tf), lambda i, j: (l, 0, j)),
                  pl.BlockSpec((None, tf, d), lambda i, j: (l, j, 0)),
                  pl.BlockSpec((None, 1, d), lambda i, j: (l, 0, 0))],
        out_specs=pl.BlockSpec((tm, d), lambda i, j: (i, 0)),
        out_shape=jax.ShapeDtypeStruct((n, d), F32),
        scratch_shapes=[pltpu.VMEM((tm, d), BF16)],
        compiler_params=_cp("parallel", "arbitrary"),
        name="mlp",
    )(x, gpre, wu, wd, gpost)


def _row_tile(n, target):
    t = min(n, target)
    while n % t:
        t //= 2
    return t


def kernel(x_prompt, x_sample, cache_k, cache_v, cache_logf, page_table, state_mlstm_C, state_mlstm_n, state_mlstm_m, state_ssd, state_conv, w_in, m_b_i, m_b_f, m_norm, f_b_f, s_conv_w, s_conv_b, s_dt_bias, s_A_log, s_D, s_norm, w_branch, w_out, ln_mix_pre, ln_mix_post, ln_mlp_pre, ln_mlp_post, w_up, w_down):
    bp, tp, d = x_prompt.shape
    bs, ts, _ = x_sample.shape
    depth = w_in.shape[0]
    mix = m_norm.shape[-1]
    mh, fh, sh = m_b_i.shape[-1], f_b_f.shape[-1], s_A_log.shape[-1]
    scd = s_conv_w.shape[-1]
    kconv = s_conv_w.shape[1]
    s_hd, ns = state_ssd.shape[-2], state_ssd.shape[-1]
    lay = _Layout(d, mix, mh, fh, sh, scd)
    assert w_in.shape[-1] == lay.in_width
    assert tp % CHUNK == 0 and ts <= SUBLANES
    n_pool, page = cache_k.shape[1], cache_k.shape[2]
    dk, dv, fhd = lay.dk, lay.dv, lay.fhd
    tsp = SUBLANES

    w_in_p = _pack_w_in(w_in, lay)
    wb = w_branch.astype(BF16)
    wo = w_out.astype(BF16)
    wu = w_up.astype(BF16)
    wd = w_down.astype(BF16)
    assert lay.small_order == ["ff", "mi", "mf", "sdt"]
    bias_small = jnp.concatenate([f_b_f, m_b_i, m_b_f, s_dt_bias,
                                  jnp.zeros((depth, LANES - lay.small_used), F32)], axis=-1)
    bias_row = bias_small[:, None, :]
    bias_col = bias_small[:, :, None]
    alog = jnp.zeros((depth, LANES), F32).at[:, lay.lane["sdt"]:lay.lane["sdt"] + sh].set(s_A_log)
    alog_row = alog[:, None, :]
    alog_col = alog[:, :, None]
    d_exp = jnp.repeat(s_D, s_hd, axis=-1)[:, None, :]
    r3 = lambda a: a[:, None, :]
    m_norm3, s_norm3, conv_b3 = r3(m_norm), r3(s_norm), r3(s_conv_b)
    g_mix_pre, g_mix_post, g_mlp_pre, g_mlp_post = r3(ln_mix_pre), r3(ln_mix_post), r3(ln_mlp_pre), r3(ln_mlp_post)

    lf_flat = cache_logf.reshape(depth * n_pool, page * fh)
    rt = _lfsuffix(lf_flat, fh, _row_tile(depth * n_pool, 512))[:, None, :]
    npg = _row_tile(page_table.shape[1], 16)

    np_, ns_ = bp * tp, bs * ts
    xp = x_prompt.reshape(np_, d)
    xs = x_sample.reshape(ns_, d)
    tm_p = _row_tile(np_, 1024)
    tm_s = _row_tile(ns_, 1024)
    tn_in = lay.width // 11 if lay.width % (11 * LANES) == 0 else LANES
    tq = _row_tile(tp, 512)

    zero_c = jnp.zeros((1, bp, mh, dv, dk), F32)
    zero_n = jnp.zeros((1, bp, mh, 1, dk), F32)
    zero_m = jnp.zeros((1, bp, mh, 1, 1), F32)
    zero_h = jnp.zeros((1, bp, sh * s_hd, ns), F32)
    c0_s = state_mlstm_C
    n0_s = state_mlstm_n[:, :, :, None, :]
    m0_s = state_mlstm_m[:, :, :, None, None]
    h0_s = state_ssd.reshape(depth, bs, sh * s_hd, ns)
    zero_cv = jnp.zeros((bp, SUBLANES, scd), F32)

    outs_p = [[] for _ in range(8)]
    outs_s = [[] for _ in range(8)]
    for l in range(depth):
        z = _inproj(xp, g_mix_pre, w_in_p, l, tm_p, tn_in)
        z3 = z.reshape(bp, tp, lay.width)
        hm, c_p, n_p, m_p, lf_p, crow = _mlstm(z3, lay, CHUNK, bias_row, bias_col, m_norm3, l,
                                               zero_c, zero_n, zero_m, 0)
        hf = _flash(z3, crow, lay, tq)
        ysd, h_p = _ssd(z3, lay, CHUNK, bias_row, bias_col, alog_row, alog_col, s_conv_w, conv_b3, d_exp, s_norm3,
                        l, zero_h, 0, zero_cv, ns, s_hd)
        mixed = _merge(hm.reshape(np_, mix), hf.reshape(np_, mix), ysd.reshape(np_, mix), wb, z, lay, l, tm_p, 512)
        xp = _outproj(mixed, wo, g_mix_post, xp, l, _row_tile(np_, 512))
        xp = _mlp(xp, g_mlp_pre, wu, wd, g_mlp_post, l, tm_p, 512)
        fk_o, fv_o, u_o = lay.dst["fk"], lay.dst["fv"], lay.dst["sxbc"]
        for lst, a in zip(outs_p, (z3[:, :, fk_o:fk_o + mix].reshape(bp, tp, fh, fhd),
                                   z3[:, :, fv_o:fv_o + mix].reshape(bp, tp, fh, fhd),
                                   lf_p, c_p, n_p.reshape(bp, mh, dk), m_p.reshape(bp, mh),
                                   h_p.reshape(bp, sh, s_hd, ns), z3[:, tp - (kconv - 1):, u_o:u_o + scd])):
            lst.append(a)

        zs = _inproj(xs, g_mix_pre, w_in_p, l, tm_s, tn_in)
        zs3 = zs.reshape(bs, ts, lay.width)
        zs8 = jnp.pad(zs3, ((0, 0), (0, tsp - ts), (0, 0)))
        hm_s, c_s, n_s, m_s, lf_s, crow_s = _mlstm(zs8, lay, ts, bias_row, bias_col, m_norm3, l,
                                                   c0_s, n0_s, m0_s, l)
        fnew = jnp.transpose(crow_s[:, :, :ts], (0, 2, 1)).reshape(bs, 1, ts * fh)
        fnew = jnp.pad(fnew, ((0, 0), (0, 0), (0, LANES - ts * fh)))
        z5 = zs8.reshape(bs, tsp, lay.width // LANES, LANES)
        hf_s = _decode(z5, fnew, cache_k, cache_v, rt, page_table, lay, l, ts, npg)
        cv0 = jnp.pad(state_conv[l], ((0, 0), (SUBLANES - (kconv - 1), 0), (0, 0)))
        ys_s, h_s = _ssd(zs8, lay, ts, bias_row, bias_col, alog_row, alog_col, s_conv_w, conv_b3, d_exp, s_norm3,
                         l, h0_s, l, cv0, ns, s_hd)
        mixed_s = _merge(hm_s[:, :ts].reshape(ns_, mix), hf_s.reshape(ns_, mix).astype(BF16),
                         ys_s[:, :ts].reshape(ns_, mix), wb, zs, lay, l, tm_s, 512)
        xs = _outproj(mixed_s, wo, g_mix_post, xs, l, tm_s)
        xs = _mlp(xs, g_mlp_pre, wu, wd, g_mlp_post, l, tm_s, 512)
        conv_full = jnp.concatenate([state_conv[l], zs3[:, :, u_o:u_o + scd]], axis=1)
        for lst, a in zip(outs_s, (zs3[:, :, fk_o:fk_o + mix].reshape(bs, ts, fh, fhd),
                                   zs3[:, :, fv_o:fv_o + mix].reshape(bs, ts, fh, fhd),
                                   lf_s[:, :ts], c_s, n_s.reshape(bs, mh, dk), m_s.reshape(bs, mh),
                                   h_s.reshape(bs, sh, s_hd, ns), conv_full[:, -(kconv - 1):])):
            lst.append(a)

    k_p, v_p, lf_pp, c_pp, n_pp, m_pp, ssd_p, conv_p = [jnp.stack(a) for a in outs_p]
    k_s, v_s, lf_ss, c_ss, n_ss, m_ss, ssd_s, conv_s = [jnp.stack(a) for a in outs_s]
    return (xp.reshape(bp, tp, d), xs.reshape(bs, ts, d), k_p, v_p, lf_pp, k_s, v_s, lf_ss,
            c_pp, n_pp, m_pp, c_ss, n_ss, m_ss, ssd_p, ssd_s, conv_p, conv_s)
```

```python
import functools

import jax
import jax.numpy as jnp
from jax import lax
from jax.experimental import pallas as pl
from jax.experimental.pallas import tpu as pltpu

F32 = jnp.float32
BF16 = jnp.bfloat16
HI = lax.Precision.HIGHEST
EPS = 1e-6
NEG_INF = float("-inf")
LOG2E = 1.4426950408889634
LANES = 128
SUBLANES = 8
CHUNK = 128
VMEM_LIMIT = 56 * 1024 * 1024


def _cp(*sem, vmem=VMEM_LIMIT):
    return pltpu.CompilerParams(dimension_semantics=sem, vmem_limit_bytes=vmem)


def _nt(a, b):
    return lax.dot_general(a, b, (((1,), (1,)), ((), ())), preferred_element_type=F32)


def _mm(a, b):
    return jnp.dot(a, b, preferred_element_type=F32)


def _mm_exact(a, b):
    return jnp.dot(a, b, preferred_element_type=F32, precision=HI)


def _softplus(x):
    return jnp.maximum(x, 0.0) + jnp.log1p(jnp.exp(-jnp.abs(x)))


def _log_sigmoid(x):
    return -_softplus(-x)


def _sigmoid(x):
    return 1.0 / (1.0 + jnp.exp(-x))


def _silu(x):
    return x * _sigmoid(x)


def _rms(x, g):
    return x * lax.rsqrt(jnp.mean(x * x, axis=-1, keepdims=True) + EPS) * g


def _pad_rows(a, rows):
    if a.shape[0] == rows:
        return a
    return jnp.concatenate([a, jnp.zeros((rows - a.shape[0], a.shape[1]), a.dtype)], axis=0)


def _imod(x, n):
    assert n & (n - 1) == 0
    return x & (n - 1)


def _idiv(x, n):
    assert n & (n - 1) == 0
    return x >> (n.bit_length() - 1)


def _pick_col(a, sel):
    return jnp.sum(jnp.where(sel, a, 0.0), axis=1, keepdims=True)


class _Layout:
    def __init__(self, d_model, mix, mh, fh, sh, scd):
        self.d, self.mix, self.mh, self.fh, self.sh, self.scd = d_model, mix, mh, fh, sh, scd
        self.dk = mix // mh // 2
        self.dv = mix // mh
        self.fhd = mix // fh
        src = [("mq", mh * self.dk), ("mk", mh * self.dk), ("mv", mix), ("mi", mh), ("mf", mh), ("mo", mix),
               ("fq", mix), ("fk", mix), ("fv", mix), ("ff", fh), ("sz", mix), ("sxbc", scd), ("sdt", sh),
               ("g", 3 * d_model)]
        self.src = {}
        off = 0
        for name, w in src:
            self.src[name] = (off, w)
            off += w
        self.in_width = off
        order = ["mq", "mk", "mv", "sxbc", "mo", "fq", "fk", "fv", "sz", "g"]
        self.dst = {}
        off = 0
        for name in order:
            self.dst[name] = off
            off += self.src[name][1]
        self.width = off
        self.small_order = ["ff", "mi", "mf", "sdt"]
        self.lane = {}
        l = 0
        for name in self.small_order:
            self.lane[name] = l
            l += self.src[name][1]
        assert l <= LANES
        self.small_used = l
        self.order = order


PACK_ROWS = 512


def _pack_kernel(tab_ref, w_ref, o_ref):
    o_ref[...] = w_ref[0].T.astype(BF16)


def _pack_w_in(w_in, lay):
    depth, d, _ = w_in.shape
    wt = jnp.swapaxes(w_in, 1, 2)
    starts = []
    for name in lay.order:
        src, width = lay.src[name]
        assert src % SUBLANES == 0 and width % PACK_ROWS == 0
        starts += [src + kblk * PACK_ROWS for kblk in range(width // PACK_ROWS)]
    table = jnp.asarray([s // SUBLANES for s in starts], jnp.int32)
    small = jnp.concatenate([wt[:, lay.src[n][0]:lay.src[n][0] + lay.src[n][1], :] for n in lay.small_order]
                            + [jnp.zeros((depth, LANES - lay.small_used, d), F32)], axis=1)
    grid_spec = pltpu.PrefetchScalarGridSpec(
        num_scalar_prefetch=1,
        grid=(depth, len(starts)),
        in_specs=[pl.BlockSpec((pl.Element(1), pl.Element(PACK_ROWS), pl.Element(d)),
                               lambda l, i, tab: (l, tab[i] * SUBLANES, 0))],
        out_specs=pl.BlockSpec((None, d, PACK_ROWS), lambda l, i, tab: (l, 0, i)),
    )
    big = pl.pallas_call(
        _pack_kernel,
        grid_spec=grid_spec,
        out_shape=jax.ShapeDtypeStruct((depth, d, lay.width), BF16),
        compiler_params=_cp("parallel", "parallel"),
        name="pack_w_in",
    )(table, wt)
    return big, jnp.swapaxes(small, 1, 2).astype(BF16)


def _inproj_kernel(x_ref, g_ref, w_ref, ws_ref, o_ref, os_ref, h_ref, *, n_big):
    j = pl.program_id(1)

    @pl.when(j == 0)
    def _():
        h_ref[...] = _rms(x_ref[...], g_ref[...]).astype(BF16)

    @pl.when(j < n_big)
    def _():
        o_ref[...] = _mm(h_ref[...], w_ref[...])

    @pl.when(j == n_big)
    def _():
        os_ref[...] = _mm(h_ref[...], ws_ref[...])


def _inproj(x, g, w, ws, l, tm, tn):
    n, d = x.shape
    width = w.shape[-1]
    n_big = width // tn
    last = n_big - 1
    return pl.pallas_call(
        functools.partial(_inproj_kernel, n_big=n_big),
        grid=(n // tm, n_big + 1),
        in_specs=[pl.BlockSpec((tm, d), lambda i, j: (i, 0)),
                  pl.BlockSpec((None, 1, d), lambda i, j: (l, 0, 0)),
                  pl.BlockSpec((None, d, tn), lambda i, j: (l, 0, jnp.minimum(j, last))),
                  pl.BlockSpec((None, d, LANES), lambda i, j: (l, 0, 0))],
        out_specs=[pl.BlockSpec((tm, tn), lambda i, j: (i, jnp.minimum(j, last))),
                   pl.BlockSpec((tm, LANES), lambda i, j: (i, 0))],
        out_shape=[jax.ShapeDtypeStruct((n, width), F32), jax.ShapeDtypeStruct((n, LANES), F32)],
        scratch_shapes=[pltpu.VMEM((tm, d), BF16)],
        compiler_params=_cp("parallel", "arbitrary"),
        name="inproj",
    )(x, g, w, ws)


def _mlstm_kernel(q_ref, k_ref, v_ref, og_ref, s_ref, brow_ref, bcol_ref, mn_ref, c0_ref, n0_ref, m0_ref,
                  hm_ref, cout_ref, nout_ref, mout_ref, lf_ref, crow_ref, ct_ref, n_ref, m_ref, fcarry_ref, *,
                  L, lr, nh, dk, dv, lane_i, lane_f, lane_ff, nfh, kscale):
    c = pl.program_id(1)
    nc = pl.num_programs(1)

    @pl.when(c == 0)
    def _():
        for h in range(nh):
            ct_ref[h] = c0_ref[0, h].T
        n_ref[...] = n0_ref[0]
        m_ref[...] = m0_ref[0]
        fcarry_ref[...] = jnp.zeros_like(fcarry_ref)

    gates = _pad_rows(s_ref[0], L)
    row = lax.broadcasted_iota(jnp.int32, (L, 1), 0)
    col = lax.broadcasted_iota(jnp.int32, (1, L), 1)
    lane = lax.broadcasted_iota(jnp.int32, (1, LANES), 1)
    valid_c = row < lr
    valid_r = col < lr
    causal = col <= row
    gb = gates + brow_ref[...]
    gt = gates.T + bcol_ref[...]
    lf = jnp.where(valid_c, _log_sigmoid(gb), 0.0)
    lft = jnp.where(valid_r, _log_sigmoid(gt), 0.0)
    bc_all = _mm_exact(causal.astype(F32), lf)
    br_all = _mm_exact(lft, (row <= col).astype(F32))
    rows_out = hm_ref.shape[1]

    lf_ref[0] = lf[:rows_out, lane_ff:lane_ff + nfh]
    csum = br_all[lane_ff:lane_ff + nfh] + fcarry_ref[...]
    crow_ref[0] = csum
    fcarry_ref[...] = csum[:, L - 1:L]

    for h in range(nh):
        q = _pad_rows(q_ref[0, :, h * dk:(h + 1) * dk], L)
        k = _pad_rows(k_ref[0, :, h * dk:(h + 1) * dk], L) * kscale
        v = _pad_rows(v_ref[0, :, h * dv:(h + 1) * dv], L)
        i_c = jnp.where(valid_c, _pick_col(gb, lane == lane_i + h), NEG_INF)
        i_r = jnp.where(valid_r, gt[lane_i + h:lane_i + h + 1, :], NEG_INF)
        b_c = _pick_col(bc_all, lane == lane_f + h)
        b_r = br_all[lane_f + h:lane_f + h + 1, :]

        m_prev = m_ref[h]
        log_d = jnp.where(causal, b_c + (i_r - b_r), NEG_INF)
        inter = b_c + m_prev
        m_t = jnp.maximum(inter, jnp.max(log_d, axis=1, keepdims=True))
        w_inter = jnp.exp(inter - m_t)
        qb = q.astype(BF16)
        kb = k.astype(BF16)
        vb = v.astype(BF16)
        s = _nt(qb, kb) * jnp.exp(log_d - m_t)
        ct = ct_ref[h]
        nvec = n_ref[h]
        num = _mm(s.astype(BF16), vb) + w_inter * _mm(qb, ct.astype(BF16))
        nq = jnp.sum(qb.astype(F32) * nvec.astype(BF16).astype(F32), axis=1, keepdims=True)
        den = jnp.sum(s, axis=1, keepdims=True) + w_inter * nq
        hh = num / jnp.maximum(jnp.abs(den), jnp.exp(-m_t))

        b_last = b_c[L - 1:L, :]
        le_c = b_last - b_c + i_c
        le_r = b_last - b_r + i_r
        m_new = jnp.maximum(b_last + m_prev, jnp.max(le_r, axis=1, keepdims=True))
        we_c = jnp.exp(le_c - m_new)
        we_r = jnp.exp(le_r - m_new)
        ws = jnp.exp(b_last + m_prev - m_new)
        ct_ref[h] = ws * ct + _mm(k.T.astype(BF16), (we_c * v).astype(BF16))
        n_ref[h] = ws * nvec + _mm(jnp.broadcast_to(we_r, (SUBLANES, L)).astype(BF16), kb)[0:1]
        m_ref[h] = m_new

        y = _rms(hh, mn_ref[:, h * dv:(h + 1) * dv]) * _sigmoid(_pad_rows(og_ref[0, :, h * dv:(h + 1) * dv], L))
        hm_ref[0, :, h * dv:(h + 1) * dv] = y[:rows_out].astype(BF16)

    @pl.when(c == nc - 1)
    def _():
        for h in range(nh):
            cout_ref[0, h] = ct_ref[h].T
        nout_ref[0] = n_ref[...]
        mout_ref[0] = m_ref[...]


def _mlstm(z3, zg3, lay, lr, bias_row, bias_col, m_norm, l, c0, n0, m0, ls):
    bsz, t, _ = z3.shape
    L = CHUNK
    rb = min(t, L)
    nc = t // rb
    mh, dk, dv = lay.mh, lay.dk, lay.dv
    fh = lay.fh
    kern = functools.partial(_mlstm_kernel, L=L, lr=lr, nh=mh, dk=dk, dv=dv, lane_i=lay.lane["mi"],
                             lane_f=lay.lane["mf"], lane_ff=lay.lane["ff"], nfh=fh, kscale=float(dk) ** -0.5)
    qw, vw = mh * dk, mh * dv
    qo, ko, vo, oo = lay.dst["mq"] // qw, lay.dst["mk"] // qw, lay.dst["mv"] // vw, lay.dst["mo"] // vw
    return pl.pallas_call(
        kern,
        grid=(bsz, nc),
        in_specs=[pl.BlockSpec((1, rb, qw), lambda b, c: (b, c, qo)),
                  pl.BlockSpec((1, rb, qw), lambda b, c: (b, c, ko)),
                  pl.BlockSpec((1, rb, vw), lambda b, c: (b, c, vo)),
                  pl.BlockSpec((1, rb, vw), lambda b, c: (b, c, oo)),
                  pl.BlockSpec((1, rb, LANES), lambda b, c: (b, c, 0)),
                  pl.BlockSpec((None, 1, LANES), lambda b, c: (l, 0, 0)),
                  pl.BlockSpec((None, LANES, 1), lambda b, c: (l, 0, 0)),
                  pl.BlockSpec((None, 1, vw), lambda b, c: (l, 0, 0)),
                  pl.BlockSpec((None, 1, mh, dv, dk), lambda b, c: (ls, b, 0, 0, 0)),
                  pl.BlockSpec((None, 1, mh, 1, dk), lambda b, c: (ls, b, 0, 0, 0)),
                  pl.BlockSpec((None, 1, mh, 1, 1), lambda b, c: (ls, b, 0, 0, 0))],
        out_specs=[pl.BlockSpec((1, rb, vw), lambda b, c: (b, c, 0)),
                   pl.BlockSpec((1, mh, dv, dk), lambda b, c: (b, 0, 0, 0)),
                   pl.BlockSpec((1, mh, 1, dk), lambda b, c: (b, 0, 0, 0)),
                   pl.BlockSpec((1, mh, 1, 1), lambda b, c: (b, 0, 0, 0)),
                   pl.BlockSpec((1, rb, fh), lambda b, c: (b, c, 0)),
                   pl.BlockSpec((1, fh, L), lambda b, c: (b, 0, c))],
        out_shape=[jax.ShapeDtypeStruct((bsz, t, vw), BF16),
                   jax.ShapeDtypeStruct((bsz, mh, dv, dk), F32),
                   jax.ShapeDtypeStruct((bsz, mh, 1, dk), F32),
                   jax.ShapeDtypeStruct((bsz, mh, 1, 1), F32),
                   jax.ShapeDtypeStruct((bsz, t, fh), F32),
                   jax.ShapeDtypeStruct((bsz, fh, nc * L), F32)],
        scratch_shapes=[pltpu.VMEM((mh, dk, dv), F32), pltpu.VMEM((mh, 1, dk), F32), pltpu.VMEM((mh, 1, 1), F32),
                        pltpu.VMEM((fh, 1), F32)],
        compiler_params=_cp("parallel", "arbitrary"),
        name="mlstm",
    )(z3, z3, z3, z3, zg3, bias_row, bias_col, m_norm, c0, n0, m0)


def _flash_kernel(q_ref, k_ref, v_ref, c_ref, o_ref, kb_ref, vb_ref, *, tq, scale):
    qi = pl.program_id(2)

    @pl.when(qi == 0)
    def _():
        kb_ref[...] = k_ref[0].astype(BF16)
        vb_ref[...] = v_ref[0].astype(BF16)

    q = (q_ref[0] * (scale * LOG2E)).astype(BF16)
    hd = q.shape[1]
    causal = lax.broadcasted_iota(jnp.int32, (1, tq), 1) <= lax.broadcasted_iota(jnp.int32, (tq, 1), 0)

    def block(j, carry, diagonal):
        m_prev, l_prev, acc = carry
        off = pl.multiple_of(j * tq, tq)
        s = _nt(q, kb_ref[pl.ds(off, tq), :]) - c_ref[0, 0, j] * LOG2E
        if diagonal:
            s = jnp.where(causal, s, NEG_INF)
        m_new = jnp.maximum(m_prev, jnp.max(s, axis=1, keepdims=True))
        alpha = jnp.exp2(m_prev - m_new)
        p = jnp.exp2(s - m_new)
        l_new = alpha * l_prev + jnp.sum(p, axis=1, keepdims=True)
        acc = alpha * acc + _mm(p.astype(BF16), vb_ref[pl.ds(off, tq), :])
        return m_new, l_new, acc

    init = (jnp.full((tq, 1), NEG_INF, F32), jnp.zeros((tq, 1), F32), jnp.zeros((tq, hd), F32))
    carry = lax.fori_loop(0, qi, lambda j, c: block(j, c, False), init)
    _, l_fin, acc = block(qi, carry, True)
    o_ref[0] = (acc / l_fin).astype(BF16)


def _flash(z3, crow, lay, tq):
    bsz, t, _ = z3.shape
    fh, hd = lay.fh, lay.fhd
    qo, ko, vo = lay.dst["fq"] // hd, lay.dst["fk"] // hd, lay.dst["fv"] // hd
    nq = t // tq
    c5 = crow.reshape(bsz, fh, nq, 1, tq)
    return pl.pallas_call(
        functools.partial(_flash_kernel, tq=tq, scale=float(hd) ** -0.5),
        grid=(bsz, fh, nq),
        in_specs=[pl.BlockSpec((1, tq, hd), lambda b, h, qi: (b, qi, qo + h)),
                  pl.BlockSpec((1, t, hd), lambda b, h, qi: (b, 0, ko + h)),
                  pl.BlockSpec((1, t, hd), lambda b, h, qi: (b, 0, vo + h)),
                  pl.BlockSpec((1, 1, nq, 1, tq), lambda b, h, qi: (b, h, 0, 0, 0))],
        out_specs=pl.BlockSpec((1, tq, hd), lambda b, h, qi: (b, qi, h)),
        out_shape=jax.ShapeDtypeStruct((bsz, t, fh * hd), BF16),
        scratch_shapes=[pltpu.VMEM((t, hd), BF16), pltpu.VMEM((t, hd), BF16)],
        compiler_params=_cp("parallel", "parallel", "arbitrary"),
        name="fox_flash",
    )(z3, z3, z3, c5)


def _lfsuffix_kernel(x_ref, o_ref, mat_ref, *, nh):
    w = x_ref.shape[1]

    @pl.when(pl.program_id(0) == 0)
    def _():
        ri = lax.broadcasted_iota(jnp.int32, (w, 1), 0)
        ci = lax.broadcasted_iota(jnp.int32, (1, w), 1)
        same = _imod(ri, nh) == _imod(ci, nh)
        mat_ref[:, 0:w] = jnp.where(same & (_idiv(ri, nh) > _idiv(ci, nh)), 1.0, 0.0).astype(BF16)
        mat_ref[:, w:2 * w] = jnp.where(same, 1.0, 0.0).astype(BF16)

    x = x_ref[...]
    x1 = x.astype(BF16)
    r1 = x - x1.astype(F32)
    x2 = r1.astype(BF16)
    x3 = (r1 - x2.astype(F32)).astype(BF16)
    mat = mat_ref[...]
    o_ref[...] = _mm(x1, mat) + (_mm(x2, mat) + _mm(x3, mat))


def _lfsuffix(lf_flat, nh, tm):
    n, w = lf_flat.shape
    return pl.pallas_call(
        functools.partial(_lfsuffix_kernel, nh=nh),
        grid=(n // tm,),
        in_specs=[pl.BlockSpec((tm, w), lambda i: (i, 0))],
        out_specs=pl.BlockSpec((tm, 2 * w), lambda i: (i, 0)),
        out_shape=jax.ShapeDtypeStruct((n, 2 * w), F32),
        scratch_shapes=[pltpu.VMEM((w, 2 * w), BF16)],
        compiler_params=_cp("arbitrary"),
        name="lf_suffix",
    )(lf_flat)


def _decode_kernel(pt_ref, q_ref, kn_ref, vn_ref, fn_ref, *refs, scale, nq, nh, npg):
    k_refs, v_refs, rt_refs = refs[0:npg], refs[npg:2 * npg], refs[2 * npg:3 * npg]
    o_ref, m_ref, l_ref, acc_ref, tail_ref = refs[3 * npg:]
    j = pl.program_id(1)
    nj = pl.num_programs(1)
    rows = nq * nh
    hd = q_ref.shape[-1]
    w = k_refs[0].shape[0] * k_refs[0].shape[1]

    @pl.when(j == 0)
    def _():
        m_ref[...] = jnp.full_like(m_ref, NEG_INF)
        l_ref[...] = jnp.zeros_like(l_ref)
        acc_ref[...] = jnp.zeros_like(acc_ref)
        tail_ref[...] = jnp.zeros_like(tail_ref)

    qs = q_ref[0].reshape(rows, hd).astype(BF16)
    row_h = _imod(lax.broadcasted_iota(jnp.int32, (rows, 1), 0), nh)

    def update(scores, values):
        m_prev = m_ref[...]
        smax = scores[0]
        for s in scores[1:]:
            smax = jnp.maximum(smax, s)
        m_new = jnp.maximum(m_prev, jnp.max(smax, axis=1, keepdims=True))
        alpha = jnp.exp(m_prev - m_new)
        psum = None
        pv = None
        for s, vb in zip(scores, values):
            p = jnp.exp(s - m_new)
            psum = p if psum is None else psum + p
            d = _mm(p.astype(BF16), vb)
            pv = d if pv is None else pv + d
        l_ref[...] = alpha * l_ref[...] + jnp.sum(psum, axis=1, keepdims=True)
        acc_ref[...] = alpha * acc_ref[...] + pv
        m_ref[...] = m_new

    same_head = _imod(lax.broadcasted_iota(jnp.int32, (1, w), 1), nh) == row_h
    tail = tail_ref[...]
    scores, values = [], []
    for i in range(npg):
        kf = k_refs[i][...].reshape(w, hd).astype(BF16)
        rt = rt_refs[i][0]
        s = _nt(qs, kf) * scale + (rt[:, 0:w] + tail)
        scores.append(jnp.where(same_head, s, NEG_INF))
        values.append(v_refs[i][...].reshape(w, hd).astype(BF16))
        tail = tail + rt[:, w:2 * w]
    tail_ref[...] = tail
    update(scores, values)

    @pl.when(j == nj - 1)
    def _():
        kn = _pad_rows(kn_ref[0].reshape(rows, hd), LANES).astype(BF16)
        vn = _pad_rows(vn_ref[0].reshape(rows, hd), LANES).astype(BF16)
        lane = lax.broadcasted_iota(jnp.int32, (1, LANES), 1)
        row_q = _idiv(lax.broadcasted_iota(jnp.int32, (rows, 1), 0), nh)
        ok = (_imod(lane, nh) == row_h) & (_idiv(lane, nh) <= row_q) & (lane < rows)
        s_new = _nt(qs, kn) * scale - fn_ref[0]
        update([jnp.where(ok, s_new, NEG_INF)], [vn])
        o_ref[0] = (acc_ref[...] / l_ref[...]).reshape(nq, nh, hd)


def _decode(z5, fnew, cache_k, cache_v, rt, page_table, lay, l, nq, npg):
    bsz = z5.shape[0]
    fh, hd = lay.fh, lay.fhd
    n_pool, page = cache_k.shape[1], cache_k.shape[2]
    n_pages = page_table.shape[1]
    w = page * fh
    qo, ko, vo = lay.dst["fq"] // (fh * hd), lay.dst["fk"] // (fh * hd), lay.dst["fv"] // (fh * hd)

    def kv_spec(i):
        return pl.BlockSpec((None, None, page, fh, hd),
                            lambda b, j, pt: (l, pt[b, n_pages - 1 - (j * npg + i)], 0, 0, 0))

    def rt_spec(i):
        return pl.BlockSpec((1, 1, 2 * w), lambda b, j, pt: (l * n_pool + pt[b, n_pages - 1 - (j * npg + i)], 0, 0))

    grid_spec = pltpu.PrefetchScalarGridSpec(
        num_scalar_prefetch=1,
        grid=(bsz, n_pages // npg),
        in_specs=[pl.BlockSpec((1, nq, fh, hd), lambda b, j, pt: (b, 0, qo, 0)),
                  pl.BlockSpec((1, nq, fh, hd), lambda b, j, pt: (b, 0, ko, 0)),
                  pl.BlockSpec((1, nq, fh, hd), lambda b, j, pt: (b, 0, vo, 0)),
                  pl.BlockSpec((1, 1, LANES), lambda b, j, pt: (b, 0, 0))]
                 + [kv_spec(i) for i in range(npg)] + [kv_spec(i) for i in range(npg)]
                 + [rt_spec(i) for i in range(npg)],
        out_specs=pl.BlockSpec((1, nq, fh, hd), lambda b, j, pt: (b, 0, 0, 0)),
        scratch_shapes=[pltpu.VMEM((nq * fh, 1), F32), pltpu.VMEM((nq * fh, 1), F32),
                        pltpu.VMEM((nq * fh, hd), F32), pltpu.VMEM((1, w), F32)],
    )
    return pl.pallas_call(
        functools.partial(_decode_kernel, scale=float(hd) ** -0.5, nq=nq, nh=fh, npg=npg),
        grid_spec=grid_spec,
        out_shape=jax.ShapeDtypeStruct((bsz, nq, fh, hd), F32),
        compiler_params=_cp("parallel", "arbitrary"),
        name="fox_decode",
    )(page_table, z5, z5, z5, fnew, *([cache_k] * npg), *([cache_v] * npg), *([rt] * npg))


def _ssd_kernel(u_ref, sz_ref, s_ref, brow_ref, bcol_ref, alrow_ref, alcol_ref, cw_ref, cb_ref, dexp_ref, sn_ref,
                h0_ref, cv0_ref, y_ref, hout_ref, xs_ref, ht_ref, *, L, lr, ng, hpg, hd, ns, lane_dt, kconv):
    c = pl.program_id(1)
    nc = pl.num_programs(1)
    mix = ng * hpg * hd
    gw = hpg * hd

    @pl.when(c == 0)
    def _():
        xs_ref[0:SUBLANES, :] = cv0_ref[0]
        for g in range(ng):
            ht_ref[g] = h0_ref[0, g * gw:(g + 1) * gw, :].T

    @pl.when(c > 0)
    def _():
        xs_ref[0:SUBLANES, :] = xs_ref[L:L + SUBLANES, :]

    xs_ref[SUBLANES:SUBLANES + L, :] = _pad_rows(u_ref[0], L)
    conv = cb_ref[...]
    for jj in range(kconv):
        off = SUBLANES - (kconv - 1) + jj
        conv = conv + cw_ref[jj:jj + 1, :] * xs_ref[pl.ds(off, L), :]
    xbc = _silu(conv)
    x = xbc[:, :mix]

    gates = _pad_rows(s_ref[0], L)
    row = lax.broadcasted_iota(jnp.int32, (L, 1), 0)
    col = lax.broadcasted_iota(jnp.int32, (1, L), 1)
    lane = lax.broadcasted_iota(jnp.int32, (1, LANES), 1)
    causal = col <= row
    dt_c = jnp.where(row < lr, _softplus(gates + brow_ref[...]), 0.0)
    dt_r = jnp.where(col < lr, _softplus(gates.T + bcol_ref[...]), 0.0)
    cum_c = _mm_exact(causal.astype(F32), dt_c * (-jnp.exp(alrow_ref[...])))
    cum_r = _mm_exact(dt_r * (-jnp.exp(alcol_ref[...])), (row <= col).astype(F32))
    lane_grp = _idiv(lax.broadcasted_iota(jnp.int32, (1, gw), 1), hd)

    ys = []
    for g in range(ng):
        bg = xbc[:, mix + g * ns:mix + (g + 1) * ns]
        cg = xbc[:, mix + (ng + g) * ns:mix + (ng + g + 1) * ns]
        bb = bg.astype(BF16)
        cb = cg.astype(BF16)
        cbm = _nt(cb, bb)
        xg = x[:, g * gw:(g + 1) * gw]
        ht = ht_ref[g]
        y = jnp.zeros((L, gw), F32)
        ecum = jnp.zeros((L, gw), F32)
        wend = jnp.zeros((L, gw), F32)
        dec = jnp.zeros((1, gw), F32)
        for r in range(hpg):
            idx = lane_dt + g * hpg + r
            cc = _pick_col(cum_c, lane == idx)
            dtc = _pick_col(dt_c, lane == idx)
            cr = cum_r[idx:idx + 1, :]
            dtr = dt_r[idx:idx + 1, :]
            wmat = cbm * jnp.exp(jnp.where(causal, cc - cr, NEG_INF)) * dtr
            sel = lane_grp == r
            y = y + _mm(wmat.astype(BF16), jnp.where(sel, xg, 0.0).astype(BF16))
            ce = cc[L - 1:L, :]
            ecum = jnp.where(sel, jnp.exp(cc), ecum)
            wend = jnp.where(sel, jnp.exp(ce - cc) * dtc, wend)
            dec = jnp.where(sel, jnp.exp(ce), dec)
        y = y + ecum * _mm(cb, ht.astype(BF16))
        ht_ref[g] = dec * ht + _mm(bg.T.astype(BF16), (wend * xg).astype(BF16))
        ys.append(y + dexp_ref[:, g * gw:(g + 1) * gw] * xg)
    yall = jnp.concatenate(ys, axis=1) * _silu(_pad_rows(sz_ref[0], L))
    y_ref[0] = _rms(yall, sn_ref[...])[:y_ref.shape[1]].astype(BF16)

    @pl.when(c == nc - 1)
    def _():
        for g in range(ng):
            hout_ref[0, g * gw:(g + 1) * gw, :] = ht_ref[g].T


def _ssd(z3, zg3, lay, lr, bias_row, bias_col, alog_row, alog_col, conv_w, conv_b, d_exp, s_norm, l, h0, ls, cv0, ns, hd):
    bsz, t, _ = z3.shape
    L = CHUNK
    rb = min(t, L)
    nc = t // rb
    mix, scd, sh = lay.mix, lay.scd, lay.sh
    ng = (scd - mix) // (2 * ns)
    hpg = sh // ng
    kconv = conv_w.shape[1]
    kern = functools.partial(_ssd_kernel, L=L, lr=lr, ng=ng, hpg=hpg, hd=hd, ns=ns, lane_dt=lay.lane["sdt"],
                             kconv=kconv)
    uo, zo = lay.dst["sxbc"] // scd, lay.dst["sz"] // mix
    return pl.pallas_call(
        kern,
        grid=(bsz, nc),
        in_specs=[pl.BlockSpec((1, rb, scd), lambda b, c: (b, c, uo)),
                  pl.BlockSpec((1, rb, mix), lambda b, c: (b, c, zo)),
                  pl.BlockSpec((1, rb, LANES), lambda b, c: (b, c, 0)),
                  pl.BlockSpec((None, 1, LANES), lambda b, c: (l, 0, 0)),
                  pl.BlockSpec((None, LANES, 1), lambda b, c: (l, 0, 0)),
                  pl.BlockSpec((None, 1, LANES), lambda b, c: (l, 0, 0)),
                  pl.BlockSpec((None, LANES, 1), lambda b, c: (l, 0, 0)),
                  pl.BlockSpec((None, kconv, scd), lambda b, c: (l, 0, 0)),
                  pl.BlockSpec((None, 1, scd), lambda b, c: (l, 0, 0)),
                  pl.BlockSpec((None, 1, mix), lambda b, c: (l, 0, 0)),
                  pl.BlockSpec((None, 1, mix), lambda b, c: (l, 0, 0)),
                  pl.BlockSpec((None, 1, sh * hd, ns), lambda b, c: (ls, b, 0, 0)),
                  pl.BlockSpec((1, SUBLANES, scd), lambda b, c: (b, 0, 0))],
        out_specs=[pl.BlockSpec((1, rb, mix), lambda b, c: (b, c, 0)),
                   pl.BlockSpec((1, sh * hd, ns), lambda b, c: (b, 0, 0))],
        out_shape=[jax.ShapeDtypeStruct((bsz, t, mix), BF16),
                   jax.ShapeDtypeStruct((bsz, sh * hd, ns), F32)],
        scratch_shapes=[pltpu.VMEM((L + 2 * SUBLANES, scd), F32), pltpu.VMEM((ng, ns, hpg * hd), F32)],
        compiler_params=_cp("parallel", "arbitrary"),
        name="ssd",
    )(z3, z3, zg3, bias_row, bias_col, alog_row, alog_col, conv_w, conv_b, d_exp, s_norm, h0, cv0)


def _merge_kernel(hm_ref, hf_ref, ys_ref, wb_ref, g0_ref, g1_ref, g2_ref, o_ref):
    acc = _sigmoid(g0_ref[...]) * _mm(hm_ref[...], wb_ref[0])
    acc = acc + _sigmoid(g1_ref[...]) * _mm(hf_ref[...], wb_ref[1])
    acc = acc + _sigmoid(g2_ref[...]) * _mm(ys_ref[...], wb_ref[2])
    o_ref[...] = acc.astype(BF16)


def _merge(hm, hf, ys, wb, z, lay, l, tm, tn):
    n, mix = hm.shape
    d = lay.d
    go = lay.dst["g"] // tn
    per = d // tn
    gspec = lambda k: pl.BlockSpec((tm, tn), lambda i, j: (i, go + k * per + j))
    act = pl.BlockSpec((tm, mix), lambda i, j: (i, 0))
    return pl.pallas_call(
        _merge_kernel,
        grid=(n // tm, d // tn),
        in_specs=[act, act, act, pl.BlockSpec((None, 3, mix, tn), lambda i, j: (l, 0, 0, j)),
                  gspec(0), gspec(1), gspec(2)],
        out_specs=pl.BlockSpec((tm, tn), lambda i, j: (i, j)),
        out_shape=jax.ShapeDtypeStruct((n, d), BF16),
        compiler_params=_cp("parallel", "arbitrary"),
        name="merge",
    )(hm, hf, ys, wb, z, z, z)


def _outproj_kernel(a_ref, w_ref, g_ref, x_ref, o_ref):
    o_ref[...] = x_ref[...] + _rms(_mm(a_ref[...], w_ref[...]), g_ref[...])


def _outproj(a, w, g, x, l, tm):
    n, d = x.shape
    return pl.pallas_call(
        _outproj_kernel,
        grid=(n // tm,),
        in_specs=[pl.BlockSpec((tm, d), lambda i: (i, 0)),
                  pl.BlockSpec((None, d, d), lambda i: (l, 0, 0)),
                  pl.BlockSpec((None, 1, d), lambda i: (l, 0, 0)),
                  pl.BlockSpec((tm, d), lambda i: (i, 0))],
        out_specs=pl.BlockSpec((tm, d), lambda i: (i, 0)),
        out_shape=jax.ShapeDtypeStruct((n, d), F32),
        compiler_params=_cp("parallel"),
        name="outproj",
    )(a, w, g, x)


def _mlp_kernel(x_ref, gpre_ref, wu_ref, wd_ref, gpost_ref, o_ref, h_ref):
    j = pl.program_id(1)

    @pl.when(j == 0)
    def _():
        h_ref[...] = _rms(x_ref[...], gpre_ref[...]).astype(BF16)
        o_ref[...] = jnp.zeros_like(o_ref)

    u = jnp.maximum(_mm(h_ref[...], wu_ref[...]), 0.0)
    o_ref[...] += _mm((u * u).astype(BF16), wd_ref[...])

    @pl.when(j == pl.num_programs(1) - 1)
    def _():
        o_ref[...] = x_ref[...] + _rms(o_ref[...], gpost_ref[...])


def _mlp(x, gpre, wu, wd, gpost, l, tm, tf):
    n, d = x.shape
    f = wu.shape[-1]
    return pl.pallas_call(
        _mlp_kernel,
        grid=(n // tm, f // tf),
        in_specs=[pl.BlockSpec((tm, d), lambda i, j: (i, 0)),
                  pl.BlockSpec((None, 1, d), lambda i, j: (l, 0, 0)),
                  pl.BlockSpec((None, d, tf), lambda i, j: (l, 0, j)),
                  pl.BlockSpec((None, tf, d), lambda i, j: (l, j, 0)),
                  pl.BlockSpec((None, 1, d), lambda i, j: (l, 0, 0))],
        out_specs=pl.BlockSpec((tm, d), lambda i, j: (i, 0)),
        out_shape=jax.ShapeDtypeStruct((n, d), F32),
        scratch_shapes=[pltpu.VMEM((tm, d), BF16)],
        compiler_params=_cp("parallel", "arbitrary"),
        name="mlp",
    )(x, gpre, wu, wd, gpost)


def _col_tile(n, target):
    best = LANES
    for t in range(2 * LANES, target + 1, 2 * LANES):
        if n % t == 0:
            best = t
    return best


def _row_tile(n, target):
    t = min(n, target)
    while n % t:
        t //= 2
    return t


def kernel(x_prompt, x_sample, cache_k, cache_v, cache_logf, page_table, state_mlstm_C, state_mlstm_n, state_mlstm_m, state_ssd, state_conv, w_in, m_b_i, m_b_f, m_norm, f_b_f, s_conv_w, s_conv_b, s_dt_bias, s_A_log, s_D, s_norm, w_branch, w_out, ln_mix_pre, ln_mix_post, ln_mlp_pre, ln_mlp_post, w_up, w_down):
    bp, tp, d = x_prompt.shape
    bs, ts, _ = x_sample.shape
    depth = w_in.shape[0]
    mix = m_norm.shape[-1]
    mh, fh, sh = m_b_i.shape[-1], f_b_f.shape[-1], s_A_log.shape[-1]
    scd = s_conv_w.shape[-1]
    kconv = s_conv_w.shape[1]
    s_hd, ns = state_ssd.shape[-2], state_ssd.shape[-1]
    lay = _Layout(d, mix, mh, fh, sh, scd)
    assert w_in.shape[-1] == lay.in_width
    assert tp % CHUNK == 0 and ts <= SUBLANES
    n_pool, page = cache_k.shape[1], cache_k.shape[2]
    dk, dv, fhd = lay.dk, lay.dv, lay.fhd
    tsp = SUBLANES

    w_in_p, w_in_g = _pack_w_in(w_in, lay)
    wb = w_branch.astype(BF16)
    wo = w_out.astype(BF16)
    wu = w_up.astype(BF16)
    wd = w_down.astype(BF16)
    assert lay.small_order == ["ff", "mi", "mf", "sdt"]
    bias_small = jnp.concatenate([f_b_f, m_b_i, m_b_f, s_dt_bias,
                                  jnp.zeros((depth, LANES - lay.small_used), F32)], axis=-1)
    bias_row = bias_small[:, None, :]
    bias_col = bias_small[:, :, None]
    alog = jnp.zeros((depth, LANES), F32).at[:, lay.lane["sdt"]:lay.lane["sdt"] + sh].set(s_A_log)
    alog_row = alog[:, None, :]
    alog_col = alog[:, :, None]
    d_exp = jnp.repeat(s_D, s_hd, axis=-1)[:, None, :]
    r3 = lambda a: a[:, None, :]
    m_norm3, s_norm3, conv_b3 = r3(m_norm), r3(s_norm), r3(s_conv_b)
    g_mix_pre, g_mix_post, g_mlp_pre, g_mlp_post = r3(ln_mix_pre), r3(ln_mix_post), r3(ln_mlp_pre), r3(ln_mlp_post)

    lf_flat = cache_logf.reshape(depth * n_pool, page * fh)
    rt = _lfsuffix(lf_flat, fh, _row_tile(depth * n_pool, 512))[:, None, :]
    npg = _row_tile(page_table.shape[1], 16)

    np_, ns_ = bp * tp, bs * ts
    xp = x_prompt.reshape(np_, d)
    xs = x_sample.reshape(ns_, d)
    tm_p = _row_tile(np_, 1024)
    tm_s = _row_tile(ns_, 1024)
    tn_in = _col_tile(lay.width, 1536)
    tq = _row_tile(tp, 512)

    zero_c = jnp.zeros((1, bp, mh, dv, dk), F32)
    zero_n = jnp.zeros((1, bp, mh, 1, dk), F32)
    zero_m = jnp.zeros((1, bp, mh, 1, 1), F32)
    zero_h = jnp.zeros((1, bp, sh * s_hd, ns), F32)
    c0_s = state_mlstm_C
    n0_s = state_mlstm_n[:, :, :, None, :]
    m0_s = state_mlstm_m[:, :, :, None, None]
    h0_s = state_ssd.reshape(depth, bs, sh * s_hd, ns)
    zero_cv = jnp.zeros((bp, SUBLANES, scd), F32)

    outs_p = [[] for _ in range(8)]
    outs_s = [[] for _ in range(8)]
    for l in range(depth):
        z, zg = _inproj(xp, g_mix_pre, w_in_p, w_in_g, l, tm_p, tn_in)
        z3 = z.reshape(bp, tp, lay.width)
        zg3 = zg.reshape(bp, tp, LANES)
        hm, c_p, n_p, m_p, lf_p, crow = _mlstm(z3, zg3, lay, CHUNK, bias_row, bias_col, m_norm3, l,
                                               zero_c, zero_n, zero_m, 0)
        hf = _flash(z3, crow, lay, tq)
        ysd, h_p = _ssd(z3, zg3, lay, CHUNK, bias_row, bias_col, alog_row, alog_col, s_conv_w, conv_b3, d_exp, s_norm3,
                        l, zero_h, 0, zero_cv, ns, s_hd)
        mixed = _merge(hm.reshape(np_, mix), hf.reshape(np_, mix), ysd.reshape(np_, mix), wb, z, lay, l, tm_p, 512)
        xp = _outproj(mixed, wo, g_mix_post, xp, l, _row_tile(np_, 512))
        xp = _mlp(xp, g_mlp_pre, wu, wd, g_mlp_post, l, tm_p, 512)
        fk_o, fv_o, u_o = lay.dst["fk"], lay.dst["fv"], lay.dst["sxbc"]
        for lst, a in zip(outs_p, (z3[:, :, fk_o:fk_o + mix].reshape(bp, tp, fh, fhd),
                                   z3[:, :, fv_o:fv_o + mix].reshape(bp, tp, fh, fhd),
                                   lf_p, c_p, n_p.reshape(bp, mh, dk), m_p.reshape(bp, mh),
                                   h_p.reshape(bp, sh, s_hd, ns), z3[:, tp - (kconv - 1):, u_o:u_o + scd])):
            lst.append(a)

        zs, zsg = _inproj(xs, g_mix_pre, w_in_p, w_in_g, l, tm_s, tn_in)
        zs3 = zs.reshape(bs, ts, lay.width)
        zs8 = jnp.pad(zs3, ((0, 0), (0, tsp - ts), (0, 0)))
        zsg8 = jnp.pad(zsg.reshape(bs, ts, LANES), ((0, 0), (0, tsp - ts), (0, 0)))
        hm_s, c_s, n_s, m_s, lf_s, crow_s = _mlstm(zs8, zsg8, lay, ts, bias_row, bias_col, m_norm3, l,
                                                   c0_s, n0_s, m0_s, l)
        fnew = jnp.transpose(crow_s[:, :, :ts], (0, 2, 1)).reshape(bs, 1, ts * fh)
        fnew = jnp.pad(fnew, ((0, 0), (0, 0), (0, LANES - ts * fh)))
        z5 = zs8.reshape(bs, tsp, lay.width // LANES, LANES)
        hf_s = _decode(z5, fnew, cache_k, cache_v, rt, page_table, lay, l, ts, npg)
        cv0 = jnp.pad(state_conv[l], ((0, 0), (SUBLANES - (kconv - 1), 0), (0, 0)))
        ys_s, h_s = _ssd(zs8, zsg8, lay, ts, bias_row, bias_col, alog_row, alog_col, s_conv_w, conv_b3, d_exp, s_norm3,
                         l, h0_s, l, cv0, ns, s_hd)
        mixed_s = _merge(hm_s[:, :ts].reshape(ns_, mix), hf_s.reshape(ns_, mix).astype(BF16),
                         ys_s[:, :ts].reshape(ns_, mix), wb, zs, lay, l, tm_s, 512)
        xs = _outproj(mixed_s, wo, g_mix_post, xs, l, tm_s)
        xs = _mlp(xs, g_mlp_pre, wu, wd, g_mlp_post, l, tm_s, 512)
        conv_full = jnp.concatenate([state_conv[l], zs3[:, :, u_o:u_o + scd]], axis=1)
        for lst, a in zip(outs_s, (zs3[:, :, fk_o:fk_o + mix].reshape(bs, ts, fh, fhd),
                                   zs3[:, :, fv_o:fv_o + mix].reshape(bs, ts, fh, fhd),
                                   lf_s[:, :ts], c_s, n_s.reshape(bs, mh, dk), m_s.reshape(bs, mh),
                                   h_s.reshape(bs, sh, s_hd, ns), conv_full[:, -(kconv - 1):])):
            lst.append(a)

    k_p, v_p, lf_pp, c_pp, n_pp, m_pp, ssd_p, conv_p = [jnp.stack(a) for a in outs_p]
    k_s, v_s, lf_ss, c_ss, n_ss, m_ss, ssd_s, conv_s = [jnp.stack(a) for a in outs_s]
    return (xp.reshape(bp, tp, d), xs.reshape(bs, ts, d), k_p, v_p, lf_pp, k_s, v_s, lf_ss,
            c_pp, n_pp, m_pp, c_ss, n_ss, m_ss, ssd_p, ssd_s, conv_p, conv_s)
```

```python
import functools

import jax
import jax.numpy as jnp
from jax import lax
from jax.experimental import pallas as pl
from jax.experimental.pallas import tpu as pltpu

F32 = jnp.float32
BF16 = jnp.bfloat16
HI = lax.Precision.HIGHEST
EPS = 1e-6
NEG_INF = float("-inf")
LOG2E = 1.4426950408889634
LANES = 128
SUBLANES = 8
CHUNK = 128
VMEM_LIMIT = 56 * 1024 * 1024


def _cp(*sem, vmem=VMEM_LIMIT):
    return pltpu.CompilerParams(dimension_semantics=sem, vmem_limit_bytes=vmem)


def _nt(a, b):
    return lax.dot_general(a, b, (((1,), (1,)), ((), ())), preferred_element_type=F32)


def _mm(a, b):
    return jnp.dot(a, b, preferred_element_type=F32)


def _mm_exact(a, b):
    return jnp.dot(a, b, preferred_element_type=F32, precision=HI)


def _softplus(x):
    return jnp.maximum(x, 0.0) + jnp.log1p(jnp.exp(-jnp.abs(x)))


def _log_sigmoid(x):
    return -_softplus(-x)


def _sigmoid(x):
    return 1.0 / (1.0 + jnp.exp(-x))


def _silu(x):
    return x * _sigmoid(x)


def _rms(x, g):
    return x * lax.rsqrt(jnp.mean(x * x, axis=-1, keepdims=True) + EPS) * g


def _pad_rows(a, rows):
    if a.shape[0] == rows:
        return a
    return jnp.concatenate([a, jnp.zeros((rows - a.shape[0], a.shape[1]), a.dtype)], axis=0)


def _imod(x, n):
    assert n & (n - 1) == 0
    return x & (n - 1)


def _idiv(x, n):
    assert n & (n - 1) == 0
    return x >> (n.bit_length() - 1)


def _pick_col(a, sel):
    return jnp.sum(jnp.where(sel, a, 0.0), axis=1, keepdims=True)


class _Layout:
    def __init__(self, d_model, mix, mh, fh, sh, scd):
        self.d, self.mix, self.mh, self.fh, self.sh, self.scd = d_model, mix, mh, fh, sh, scd
        self.dk = mix // mh // 2
        self.dv = mix // mh
        self.fhd = mix // fh
        src = [("mq", mh * self.dk), ("mk", mh * self.dk), ("mv", mix), ("mi", mh), ("mf", mh), ("mo", mix),
               ("fq", mix), ("fk", mix), ("fv", mix), ("ff", fh), ("sz", mix), ("sxbc", scd), ("sdt", sh),
               ("g", 3 * d_model)]
        self.src = {}
        off = 0
        for name, w in src:
            self.src[name] = (off, w)
            off += w
        self.in_width = off
        order = ["mq", "mk", "mv", "sxbc", "mo", "fq", "fk", "fv", "sz", "g"]
        self.dst = {}
        off = 0
        for name in order:
            self.dst[name] = off
            off += self.src[name][1]
        self.width = off
        self.small_order = ["ff", "mi", "mf", "sdt"]
        self.lane = {}
        l = 0
        for name in self.small_order:
            self.lane[name] = l
            l += self.src[name][1]
        assert l <= LANES
        self.small_used = l
        self.order = order


PACK_ROWS = 512


def _pack_kernel(tab_ref, w_ref, o_ref):
    o_ref[...] = w_ref[0].T.astype(BF16)


def _pack_w_in(w_in, lay):
    depth, d, _ = w_in.shape
    wt = jnp.swapaxes(w_in, 1, 2)
    starts = []
    for name in lay.order:
        src, width = lay.src[name]
        assert src % SUBLANES == 0 and width % PACK_ROWS == 0
        starts += [src + kblk * PACK_ROWS for kblk in range(width // PACK_ROWS)]
    table = jnp.asarray([s // SUBLANES for s in starts], jnp.int32)
    small = jnp.concatenate([w_in[:, :, lay.src[n][0]:lay.src[n][0] + lay.src[n][1]] for n in lay.small_order]
                            + [jnp.zeros((depth, d, LANES - lay.small_used), F32)], axis=2)
    grid_spec = pltpu.PrefetchScalarGridSpec(
        num_scalar_prefetch=1,
        grid=(depth, len(starts)),
        in_specs=[pl.BlockSpec((pl.Element(1), pl.Element(PACK_ROWS), pl.Element(d)),
                               lambda l, i, tab: (l, tab[i] * SUBLANES, 0))],
        out_specs=pl.BlockSpec((None, d, PACK_ROWS), lambda l, i, tab: (l, 0, i)),
    )
    big = pl.pallas_call(
        _pack_kernel,
        grid_spec=grid_spec,
        out_shape=jax.ShapeDtypeStruct((depth, d, lay.width), BF16),
        compiler_params=_cp("parallel", "parallel"),
        name="pack_w_in",
    )(table, wt)
    return big, small


def _inproj_kernel(x_ref, g_ref, w_ref, ws_ref, o_ref, os_ref, h_ref, *, n_big):
    j = pl.program_id(1)

    @pl.when(j == 0)
    def _():
        h_ref[...] = _rms(x_ref[...], g_ref[...]).astype(BF16)

    @pl.when(j < n_big)
    def _():
        o_ref[...] = _mm(h_ref[...], w_ref[...])

    @pl.when(j == n_big)
    def _():
        os_ref[...] = _mm(h_ref[...], ws_ref[...].astype(BF16))


def _inproj(x, g, w, ws, l, tm, tn):
    n, d = x.shape
    width = w.shape[-1]
    n_big = width // tn
    last = n_big - 1
    return pl.pallas_call(
        functools.partial(_inproj_kernel, n_big=n_big),
        grid=(n // tm, n_big + 1),
        in_specs=[pl.BlockSpec((tm, d), lambda i, j: (i, 0)),
                  pl.BlockSpec((None, 1, d), lambda i, j: (l, 0, 0)),
                  pl.BlockSpec((None, d, tn), lambda i, j: (l, 0, jnp.minimum(j, last))),
                  pl.BlockSpec((None, d, LANES), lambda i, j: (l, 0, 0))],
        out_specs=[pl.BlockSpec((tm, tn), lambda i, j: (i, jnp.minimum(j, last))),
                   pl.BlockSpec((tm, LANES), lambda i, j: (i, 0))],
        out_shape=[jax.ShapeDtypeStruct((n, width), F32), jax.ShapeDtypeStruct((n, LANES), F32)],
        scratch_shapes=[pltpu.VMEM((tm, d), BF16)],
        compiler_params=_cp("parallel", "arbitrary"),
        name="inproj",
    )(x, g, w, ws)


def _mlstm_kernel(q_ref, k_ref, v_ref, og_ref, s_ref, brow_ref, bcol_ref, mn_ref, c0_ref, n0_ref, m0_ref,
                  hm_ref, cout_ref, nout_ref, mout_ref, lf_ref, crow_ref, ct_ref, n_ref, m_ref, fcarry_ref, *,
                  L, lr, nh, dk, dv, lane_i, lane_f, lane_ff, nfh, kscale):
    c = pl.program_id(1)
    nc = pl.num_programs(1)

    @pl.when(c == 0)
    def _():
        for h in range(nh):
            ct_ref[h] = c0_ref[0, h].T
        n_ref[...] = n0_ref[0]
        m_ref[...] = m0_ref[0]
        fcarry_ref[...] = jnp.zeros_like(fcarry_ref)

    gates = _pad_rows(s_ref[0], L)
    row = lax.broadcasted_iota(jnp.int32, (L, 1), 0)
    col = lax.broadcasted_iota(jnp.int32, (1, L), 1)
    lane = lax.broadcasted_iota(jnp.int32, (1, LANES), 1)
    valid_c = row < lr
    valid_r = col < lr
    causal = col <= row
    gb = gates + brow_ref[...]
    gt = gates.T + bcol_ref[...]
    lf = jnp.where(valid_c, _log_sigmoid(gb), 0.0)
    lft = jnp.where(valid_r, _log_sigmoid(gt), 0.0)
    bc_all = _mm_exact(causal.astype(F32), lf)
    br_all = _mm_exact(lft, (row <= col).astype(F32))
    rows_out = hm_ref.shape[1]

    lf_ref[0] = lf[:rows_out, lane_ff:lane_ff + nfh]
    csum = br_all[lane_ff:lane_ff + nfh] + fcarry_ref[...]
    crow_ref[0] = csum
    fcarry_ref[...] = csum[:, L - 1:L]

    for h in range(nh):
        q = _pad_rows(q_ref[0, :, h * dk:(h + 1) * dk], L)
        k = _pad_rows(k_ref[0, :, h * dk:(h + 1) * dk], L) * kscale
        v = _pad_rows(v_ref[0, :, h * dv:(h + 1) * dv], L)
        i_c = jnp.where(valid_c, _pick_col(gb, lane == lane_i + h), NEG_INF)
        i_r = jnp.where(valid_r, gt[lane_i + h:lane_i + h + 1, :], NEG_INF)
        b_c = _pick_col(bc_all, lane == lane_f + h)
        b_r = br_all[lane_f + h:lane_f + h + 1, :]

        m_prev = m_ref[h]
        log_d = jnp.where(causal, b_c + (i_r - b_r), NEG_INF)
        inter = b_c + m_prev
        m_t = jnp.maximum(inter, jnp.max(log_d, axis=1, keepdims=True))
        w_inter = jnp.exp(inter - m_t)
        qb = q.astype(BF16)
        kb = k.astype(BF16)
        vb = v.astype(BF16)
        s = _nt(qb, kb) * jnp.exp(log_d - m_t)
        ct = ct_ref[h]
        nvec = n_ref[h]
        num = _mm(s.astype(BF16), vb) + w_inter * _mm(qb, ct.astype(BF16))
        nq = jnp.sum(qb.astype(F32) * nvec.astype(BF16).astype(F32), axis=1, keepdims=True)
        den = jnp.sum(s, axis=1, keepdims=True) + w_inter * nq
        hh = num / jnp.maximum(jnp.abs(den), jnp.exp(-m_t))

        b_last = b_c[L - 1:L, :]
        le_c = b_last - b_c + i_c
        le_r = b_last - b_r + i_r
        m_new = jnp.maximum(b_last + m_prev, jnp.max(le_r, axis=1, keepdims=True))
        we_c = jnp.exp(le_c - m_new)
        we_r = jnp.exp(le_r - m_new)
        ws = jnp.exp(b_last + m_prev - m_new)
        ct_ref[h] = ws * ct + _mm(k.T.astype(BF16), (we_c * v).astype(BF16))
        n_ref[h] = ws * nvec + _mm(jnp.broadcast_to(we_r, (SUBLANES, L)).astype(BF16), kb)[0:1]
        m_ref[h] = m_new

        y = _rms(hh, mn_ref[:, h * dv:(h + 1) * dv]) * _sigmoid(_pad_rows(og_ref[0, :, h * dv:(h + 1) * dv], L))
        hm_ref[0, :, h * dv:(h + 1) * dv] = y[:rows_out].astype(BF16)

    @pl.when(c == nc - 1)
    def _():
        for h in range(nh):
            cout_ref[0, h] = ct_ref[h].T
        nout_ref[0] = n_ref[...]
        mout_ref[0] = m_ref[...]


def _mlstm(z3, zg3, lay, lr, bias_row, bias_col, m_norm, l, c0, n0, m0, ls):
    bsz, t, _ = z3.shape
    L = CHUNK
    rb = min(t, L)
    nc = t // rb
    mh, dk, dv = lay.mh, lay.dk, lay.dv
    fh = lay.fh
    kern = functools.partial(_mlstm_kernel, L=L, lr=lr, nh=mh, dk=dk, dv=dv, lane_i=lay.lane["mi"],
                             lane_f=lay.lane["mf"], lane_ff=lay.lane["ff"], nfh=fh, kscale=float(dk) ** -0.5)
    qw, vw = mh * dk, mh * dv
    qo, ko, vo, oo = lay.dst["mq"] // qw, lay.dst["mk"] // qw, lay.dst["mv"] // vw, lay.dst["mo"] // vw
    return pl.pallas_call(
        kern,
        grid=(bsz, nc),
        in_specs=[pl.BlockSpec((1, rb, qw), lambda b, c: (b, c, qo)),
                  pl.BlockSpec((1, rb, qw), lambda b, c: (b, c, ko)),
                  pl.BlockSpec((1, rb, vw), lambda b, c: (b, c, vo)),
                  pl.BlockSpec((1, rb, vw), lambda b, c: (b, c, oo)),
                  pl.BlockSpec((1, rb, LANES), lambda b, c: (b, c, 0)),
                  pl.BlockSpec((None, 1, LANES), lambda b, c: (l, 0, 0)),
                  pl.BlockSpec((None, LANES, 1), lambda b, c: (l, 0, 0)),
                  pl.BlockSpec((None, 1, vw), lambda b, c: (l, 0, 0)),
                  pl.BlockSpec((None, 1, mh, dv, dk), lambda b, c: (ls, b, 0, 0, 0)),
                  pl.BlockSpec((None, 1, mh, 1, dk), lambda b, c: (ls, b, 0, 0, 0)),
                  pl.BlockSpec((None, 1, mh, 1, 1), lambda b, c: (ls, b, 0, 0, 0))],
        out_specs=[pl.BlockSpec((1, rb, vw), lambda b, c: (b, c, 0)),
                   pl.BlockSpec((1, mh, dv, dk), lambda b, c: (b, 0, 0, 0)),
                   pl.BlockSpec((1, mh, 1, dk), lambda b, c: (b, 0, 0, 0)),
                   pl.BlockSpec((1, mh, 1, 1), lambda b, c: (b, 0, 0, 0)),
                   pl.BlockSpec((1, rb, fh), lambda b, c: (b, c, 0)),
                   pl.BlockSpec((1, fh, L), lambda b, c: (b, 0, c))],
        out_shape=[jax.ShapeDtypeStruct((bsz, t, vw), BF16),
                   jax.ShapeDtypeStruct((bsz, mh, dv, dk), F32),
                   jax.ShapeDtypeStruct((bsz, mh, 1, dk), F32),
                   jax.ShapeDtypeStruct((bsz, mh, 1, 1), F32),
                   jax.ShapeDtypeStruct((bsz, t, fh), F32),
                   jax.ShapeDtypeStruct((bsz, fh, nc * L), F32)],
        scratch_shapes=[pltpu.VMEM((mh, dk, dv), F32), pltpu.VMEM((mh, 1, dk), F32), pltpu.VMEM((mh, 1, 1), F32),
                        pltpu.VMEM((fh, 1), F32)],
        compiler_params=_cp("parallel", "arbitrary"),
        name="mlstm",
    )(z3, z3, z3, z3, zg3, bias_row, bias_col, m_norm, c0, n0, m0)


def _flash_kernel(q_ref, k_ref, v_ref, c_ref, o_ref, kb_ref, vb_ref, *, tq, scale):
    qi = pl.program_id(2)

    @pl.when(qi == 0)
    def _():
        kb_ref[...] = k_ref[0].astype(BF16)
        vb_ref[...] = v_ref[0].astype(BF16)

    q = (q_ref[0] * (scale * LOG2E)).astype(BF16)
    hd = q.shape[1]
    causal = lax.broadcasted_iota(jnp.int32, (1, tq), 1) <= lax.broadcasted_iota(jnp.int32, (tq, 1), 0)

    def block(j, carry, diagonal):
        m_prev, l_prev, acc = carry
        off = pl.multiple_of(j * tq, tq)
        s = _nt(q, kb_ref[pl.ds(off, tq), :]) - c_ref[0, 0, j] * LOG2E
        if diagonal:
            s = jnp.where(causal, s, NEG_INF)
        m_new = jnp.maximum(m_prev, jnp.max(s, axis=1, keepdims=True))
        alpha = jnp.exp2(m_prev - m_new)
        p = jnp.exp2(s - m_new)
        l_new = alpha * l_prev + jnp.sum(p, axis=1, keepdims=True)
        acc = alpha * acc + _mm(p.astype(BF16), vb_ref[pl.ds(off, tq), :])
        return m_new, l_new, acc

    init = (jnp.full((tq, 1), NEG_INF, F32), jnp.zeros((tq, 1), F32), jnp.zeros((tq, hd), F32))
    carry = lax.fori_loop(0, qi, lambda j, c: block(j, c, False), init)
    _, l_fin, acc = block(qi, carry, True)
    o_ref[0] = (acc / l_fin).astype(BF16)


def _flash(z3, crow, lay, tq):
    bsz, t, _ = z3.shape
    fh, hd = lay.fh, lay.fhd
    qo, ko, vo = lay.dst["fq"] // hd, lay.dst["fk"] // hd, lay.dst["fv"] // hd
    nq = t // tq
    c5 = crow.reshape(bsz, fh, nq, 1, tq)
    return pl.pallas_call(
        functools.partial(_flash_kernel, tq=tq, scale=float(hd) ** -0.5),
        grid=(bsz, fh, nq),
        in_specs=[pl.BlockSpec((1, tq, hd), lambda b, h, qi: (b, qi, qo + h)),
                  pl.BlockSpec((1, t, hd), lambda b, h, qi: (b, 0, ko + h)),
                  pl.BlockSpec((1, t, hd), lambda b, h, qi: (b, 0, vo + h)),
                  pl.BlockSpec((1, 1, nq, 1, tq), lambda b, h, qi: (b, h, 0, 0, 0))],
        out_specs=pl.BlockSpec((1, tq, hd), lambda b, h, qi: (b, qi, h)),
        out_shape=jax.ShapeDtypeStruct((bsz, t, fh * hd), BF16),
        scratch_shapes=[pltpu.VMEM((t, hd), BF16), pltpu.VMEM((t, hd), BF16)],
        compiler_params=_cp("parallel", "parallel", "arbitrary"),
        name="fox_flash",
    )(z3, z3, z3, c5)


def _lfsuffix_kernel(x_ref, o_ref, mat_ref, *, nh):
    w = x_ref.shape[1]

    @pl.when(pl.program_id(0) == 0)
    def _():
        ri = lax.broadcasted_iota(jnp.int32, (w, 1), 0)
        ci = lax.broadcasted_iota(jnp.int32, (1, w), 1)
        same = _imod(ri, nh) == _imod(ci, nh)
        mat_ref[:, 0:w] = jnp.where(same & (_idiv(ri, nh) > _idiv(ci, nh)), 1.0, 0.0).astype(BF16)
        mat_ref[:, w:2 * w] = jnp.where(same, 1.0, 0.0).astype(BF16)

    x = x_ref[...]
    x1 = x.astype(BF16)
    r1 = x - x1.astype(F32)
    x2 = r1.astype(BF16)
    x3 = (r1 - x2.astype(F32)).astype(BF16)
    mat = mat_ref[...]
    o_ref[...] = _mm(x1, mat) + (_mm(x2, mat) + _mm(x3, mat))


def _lfsuffix(lf_flat, nh, tm):
    n, w = lf_flat.shape
    return pl.pallas_call(
        functools.partial(_lfsuffix_kernel, nh=nh),
        grid=(n // tm,),
        in_specs=[pl.BlockSpec((tm, w), lambda i: (i, 0))],
        out_specs=pl.BlockSpec((tm, 2 * w), lambda i: (i, 0)),
        out_shape=jax.ShapeDtypeStruct((n, 2 * w), F32),
        scratch_shapes=[pltpu.VMEM((w, 2 * w), BF16)],
        compiler_params=_cp("arbitrary"),
        name="lf_suffix",
    )(lf_flat)


def _decode_kernel(pt_ref, q_ref, kn_ref, vn_ref, fn_ref, *refs, scale, nq, nh, npg):
    k_refs, v_refs, rt_refs = refs[0:npg], refs[npg:2 * npg], refs[2 * npg:3 * npg]
    o_ref, m_ref, l_ref, acc_ref, tail_ref = refs[3 * npg:]
    j = pl.program_id(1)
    nj = pl.num_programs(1)
    rows = nq * nh
    hd = q_ref.shape[-1]
    w = k_refs[0].shape[0] * k_refs[0].shape[1]

    @pl.when(j == 0)
    def _():
        m_ref[...] = jnp.full_like(m_ref, NEG_INF)
        l_ref[...] = jnp.zeros_like(l_ref)
        acc_ref[...] = jnp.zeros_like(acc_ref)
        tail_ref[...] = jnp.zeros_like(tail_ref)

    qs = q_ref[0].reshape(rows, hd).astype(BF16)
    row_h = _imod(lax.broadcasted_iota(jnp.int32, (rows, 1), 0), nh)

    def update(scores, values):
        m_prev = m_ref[...]
        smax = scores[0]
        for s in scores[1:]:
            smax = jnp.maximum(smax, s)
        m_new = jnp.maximum(m_prev, jnp.max(smax, axis=1, keepdims=True))
        alpha = jnp.exp(m_prev - m_new)
        psum = None
        pv = None
        for s, vb in zip(scores, values):
            p = jnp.exp(s - m_new)
            psum = p if psum is None else psum + p
            d = _mm(p.astype(BF16), vb)
            pv = d if pv is None else pv + d
        l_ref[...] = alpha * l_ref[...] + jnp.sum(psum, axis=1, keepdims=True)
        acc_ref[...] = alpha * acc_ref[...] + pv
        m_ref[...] = m_new

    same_head = _imod(lax.broadcasted_iota(jnp.int32, (1, w), 1), nh) == row_h
    tail = tail_ref[...]
    scores, values = [], []
    for i in range(npg):
        kf = k_refs[i][...].reshape(w, hd).astype(BF16)
        rt = rt_refs[i][0]
        s = _nt(qs, kf) * scale + (rt[:, 0:w] + tail)
        scores.append(jnp.where(same_head, s, NEG_INF))
        values.append(v_refs[i][...].reshape(w, hd).astype(BF16))
        tail = tail + rt[:, w:2 * w]
    tail_ref[...] = tail
    update(scores, values)

    @pl.when(j == nj - 1)
    def _():
        kn = _pad_rows(kn_ref[0].reshape(rows, hd), LANES).astype(BF16)
        vn = _pad_rows(vn_ref[0].reshape(rows, hd), LANES).astype(BF16)
        lane = lax.broadcasted_iota(jnp.int32, (1, LANES), 1)
        row_q = _idiv(lax.broadcasted_iota(jnp.int32, (rows, 1), 0), nh)
        ok = (_imod(lane, nh) == row_h) & (_idiv(lane, nh) <= row_q) & (lane < rows)
        s_new = _nt(qs, kn) * scale - fn_ref[0]
        update([jnp.where(ok, s_new, NEG_INF)], [vn])
        o_ref[0] = (acc_ref[...] / l_ref[...]).reshape(nq, nh, hd)


def _decode(z5, fnew, cache_k, cache_v, rt, page_table, lay, l, nq, npg):
    bsz = z5.shape[0]
    fh, hd = lay.fh, lay.fhd
    n_pool, page = cache_k.shape[1], cache_k.shape[2]
    n_pages = page_table.shape[1]
    w = page * fh
    qo, ko, vo = lay.dst["fq"] // (fh * hd), lay.dst["fk"] // (fh * hd), lay.dst["fv"] // (fh * hd)

    def kv_spec(i):
        return pl.BlockSpec((None, None, page, fh, hd),
                            lambda b, j, pt: (l, pt[b, n_pages - 1 - (j * npg + i)], 0, 0, 0))

    def rt_spec(i):
        return pl.BlockSpec((1, 1, 2 * w), lambda b, j, pt: (l * n_pool + pt[b, n_pages - 1 - (j * npg + i)], 0, 0))

    grid_spec = pltpu.PrefetchScalarGridSpec(
        num_scalar_prefetch=1,
        grid=(bsz, n_pages // npg),
        in_specs=[pl.BlockSpec((1, nq, fh, hd), lambda b, j, pt: (b, 0, qo, 0)),
                  pl.BlockSpec((1, nq, fh, hd), lambda b, j, pt: (b, 0, ko, 0)),
                  pl.BlockSpec((1, nq, fh, hd), lambda b, j, pt: (b, 0, vo, 0)),
                  pl.BlockSpec((1, 1, LANES), lambda b, j, pt: (b, 0, 0))]
                 + [kv_spec(i) for i in range(npg)] + [kv_spec(i) for i in range(npg)]
                 + [rt_spec(i) for i in range(npg)],
        out_specs=pl.BlockSpec((1, nq, fh, hd), lambda b, j, pt: (b, 0, 0, 0)),
        scratch_shapes=[pltpu.VMEM((nq * fh, 1), F32), pltpu.VMEM((nq * fh, 1), F32),
                        pltpu.VMEM((nq * fh, hd), F32), pltpu.VMEM((1, w), F32)],
    )
    return pl.pallas_call(
        functools.partial(_decode_kernel, scale=float(hd) ** -0.5, nq=nq, nh=fh, npg=npg),
        grid_spec=grid_spec,
        out_shape=jax.ShapeDtypeStruct((bsz, nq, fh, hd), F32),
        compiler_params=_cp("parallel", "arbitrary"),
        name="fox_decode",
    )(page_table, z5, z5, z5, fnew, *([cache_k] * npg), *([cache_v] * npg), *([rt] * npg))


def _ssd_kernel(u_ref, sz_ref, s_ref, brow_ref, bcol_ref, alrow_ref, alcol_ref, cw_ref, cb_ref, dexp_ref, sn_ref,
                h0_ref, cv0_ref, y_ref, hout_ref, xs_ref, ht_ref, *, L, lr, ng, hpg, hd, ns, lane_dt, kconv):
    c = pl.program_id(1)
    nc = pl.num_programs(1)
    mix = ng * hpg * hd
    gw = hpg * hd

    @pl.when(c == 0)
    def _():
        xs_ref[0:SUBLANES, :] = cv0_ref[0]
        for g in range(ng):
            ht_ref[g] = h0_ref[0, g * gw:(g + 1) * gw, :].T

    @pl.when(c > 0)
    def _():
        xs_ref[0:SUBLANES, :] = xs_ref[L:L + SUBLANES, :]

    xs_ref[SUBLANES:SUBLANES + L, :] = _pad_rows(u_ref[0], L)
    conv = cb_ref[...]
    for jj in range(kconv):
        off = SUBLANES - (kconv - 1) + jj
        conv = conv + cw_ref[jj:jj + 1, :] * xs_ref[pl.ds(off, L), :]
    xbc = _silu(conv)
    x = xbc[:, :mix]

    gates = _pad_rows(s_ref[0], L)
    row = lax.broadcasted_iota(jnp.int32, (L, 1), 0)
    col = lax.broadcasted_iota(jnp.int32, (1, L), 1)
    lane = lax.broadcasted_iota(jnp.int32, (1, LANES), 1)
    causal = col <= row
    dt_c = jnp.where(row < lr, _softplus(gates + brow_ref[...]), 0.0)
    dt_r = jnp.where(col < lr, _softplus(gates.T + bcol_ref[...]), 0.0)
    cum_c = _mm_exact(causal.astype(F32), dt_c * (-jnp.exp(alrow_ref[...])))
    cum_r = _mm_exact(dt_r * (-jnp.exp(alcol_ref[...])), (row <= col).astype(F32))
    lane_grp = _idiv(lax.broadcasted_iota(jnp.int32, (1, gw), 1), hd)

    ys = []
    for g in range(ng):
        bg = xbc[:, mix + g * ns:mix + (g + 1) * ns]
        cg = xbc[:, mix + (ng + g) * ns:mix + (ng + g + 1) * ns]
        bb = bg.astype(BF16)
        cb = cg.astype(BF16)
        cbm = _nt(cb, bb)
        xg = x[:, g * gw:(g + 1) * gw]
        ht = ht_ref[g]
        y = jnp.zeros((L, gw), F32)
        ecum = jnp.zeros((L, gw), F32)
        wend = jnp.zeros((L, gw), F32)
        dec = jnp.zeros((1, gw), F32)
        for r in range(hpg):
            idx = lane_dt + g * hpg + r
            cc = _pick_col(cum_c, lane == idx)
            dtc = _pick_col(dt_c, lane == idx)
            cr = cum_r[idx:idx + 1, :]
            dtr = dt_r[idx:idx + 1, :]
            wmat = cbm * jnp.exp(jnp.where(causal, cc - cr, NEG_INF)) * dtr
            sel = lane_grp == r
            y = y + _mm(wmat.astype(BF16), jnp.where(sel, xg, 0.0).astype(BF16))
            ce = cc[L - 1:L, :]
            ecum = jnp.where(sel, jnp.exp(cc), ecum)
            wend = jnp.where(sel, jnp.exp(ce - cc) * dtc, wend)
            dec = jnp.where(sel, jnp.exp(ce), dec)
        y = y + ecum * _mm(cb, ht.astype(BF16))
        ht_ref[g] = dec * ht + _mm(bg.T.astype(BF16), (wend * xg).astype(BF16))
        ys.append(y + dexp_ref[:, g * gw:(g + 1) * gw] * xg)
    yall = jnp.concatenate(ys, axis=1) * _silu(_pad_rows(sz_ref[0], L))
    y_ref[0] = _rms(yall, sn_ref[...])[:y_ref.shape[1]].astype(BF16)

    @pl.when(c == nc - 1)
    def _():
        for g in range(ng):
            hout_ref[0, g * gw:(g + 1) * gw, :] = ht_ref[g].T


def _ssd(z3, zg3, lay, lr, bias_row, bias_col, alog_row, alog_col, conv_w, conv_b, d_exp, s_norm, l, h0, ls, cv0, ns, hd):
    bsz, t, _ = z3.shape
    L = CHUNK
    rb = min(t, L)
    nc = t // rb
    mix, scd, sh = lay.mix, lay.scd, lay.sh
    ng = (scd - mix) // (2 * ns)
    hpg = sh // ng
    kconv = conv_w.shape[1]
    kern = functools.partial(_ssd_kernel, L=L, lr=lr, ng=ng, hpg=hpg, hd=hd, ns=ns, lane_dt=lay.lane["sdt"],
                             kconv=kconv)
    uo, zo = lay.dst["sxbc"] // scd, lay.dst["sz"] // mix
    return pl.pallas_call(
        kern,
        grid=(bsz, nc),
        in_specs=[pl.BlockSpec((1, rb, scd), lambda b, c: (b, c, uo)),
                  pl.BlockSpec((1, rb, mix), lambda b, c: (b, c, zo)),
                  pl.BlockSpec((1, rb, LANES), lambda b, c: (b, c, 0)),
                  pl.BlockSpec((None, 1, LANES), lambda b, c: (l, 0, 0)),
                  pl.BlockSpec((None, LANES, 1), lambda b, c: (l, 0, 0)),
                  pl.BlockSpec((None, 1, LANES), lambda b, c: (l, 0, 0)),
                  pl.BlockSpec((None, LANES, 1), lambda b, c: (l, 0, 0)),
                  pl.BlockSpec((None, kconv, scd), lambda b, c: (l, 0, 0)),
                  pl.BlockSpec((None, 1, scd), lambda b, c: (l, 0, 0)),
                  pl.BlockSpec((None, 1, mix), lambda b, c: (l, 0, 0)),
                  pl.BlockSpec((None, 1, mix), lambda b, c: (l, 0, 0)),
                  pl.BlockSpec((None, 1, sh * hd, ns), lambda b, c: (ls, b, 0, 0)),
                  pl.BlockSpec((1, SUBLANES, scd), lambda b, c: (b, 0, 0))],
        out_specs=[pl.BlockSpec((1, rb, mix), lambda b, c: (b, c, 0)),
                   pl.BlockSpec((1, sh * hd, ns), lambda b, c: (b, 0, 0))],
        out_shape=[jax.ShapeDtypeStruct((bsz, t, mix), BF16),
                   jax.ShapeDtypeStruct((bsz, sh * hd, ns), F32)],
        scratch_shapes=[pltpu.VMEM((L + 2 * SUBLANES, scd), F32), pltpu.VMEM((ng, ns, hpg * hd), F32)],
        compiler_params=_cp("parallel", "arbitrary"),
        name="ssd",
    )(z3, z3, zg3, bias_row, bias_col, alog_row, alog_col, conv_w, conv_b, d_exp, s_norm, h0, cv0)


def _merge_kernel(hm_ref, hf_ref, ys_ref, wb_ref, g0_ref, g1_ref, g2_ref, o_ref):
    acc = _sigmoid(g0_ref[...]) * _mm(hm_ref[...], wb_ref[0])
    acc = acc + _sigmoid(g1_ref[...]) * _mm(hf_ref[...], wb_ref[1])
    acc = acc + _sigmoid(g2_ref[...]) * _mm(ys_ref[...], wb_ref[2])
    o_ref[...] = acc.astype(BF16)


def _merge(hm, hf, ys, wb, z, lay, l, tm, tn):
    n, mix = hm.shape
    d = lay.d
    go = lay.dst["g"] // tn
    per = d // tn
    gspec = lambda k: pl.BlockSpec((tm, tn), lambda i, j: (i, go + k * per + j))
    act = pl.BlockSpec((tm, mix), lambda i, j: (i, 0))
    return pl.pallas_call(
        _merge_kernel,
        grid=(n // tm, d // tn),
        in_specs=[act, act, act, pl.BlockSpec((None, 3, mix, tn), lambda i, j: (l, 0, 0, j)),
                  gspec(0), gspec(1), gspec(2)],
        out_specs=pl.BlockSpec((tm, tn), lambda i, j: (i, j)),
        out_shape=jax.ShapeDtypeStruct((n, d), BF16),
        compiler_params=_cp("parallel", "arbitrary"),
        name="merge",
    )(hm, hf, ys, wb, z, z, z)


def _outproj_kernel(a_ref, w_ref, g_ref, x_ref, o_ref):
    o_ref[...] = x_ref[...] + _rms(_mm(a_ref[...], w_ref[...]), g_ref[...])


def _outproj(a, w, g, x, l, tm):
    n, d = x.shape
    return pl.pallas_call(
        _outproj_kernel,
        grid=(n // tm,),
        in_specs=[pl.BlockSpec((tm, d), lambda i: (i, 0)),
                  pl.BlockSpec((None, d, d), lambda i: (l, 0, 0)),
                  pl.BlockSpec((None, 1, d), lambda i: (l, 0, 0)),
                  pl.BlockSpec((tm, d), lambda i: (i, 0))],
        out_specs=pl.BlockSpec((tm, d), lambda i: (i, 0)),
        out_shape=jax.ShapeDtypeStruct((n, d), F32),
        compiler_params=_cp("parallel"),
        name="outproj",
    )(a, w, g, x)


def _mlp_kernel(x_ref, gpre_ref, wu_ref, wd_ref, gpost_ref, o_ref, h_ref):
    j = pl.program_id(1)

    @pl.when(j == 0)
    def _():
        h_ref[...] = _rms(x_ref[...], gpre_ref[...]).astype(BF16)
        o_ref[...] = jnp.zeros_like(o_ref)

    u = jnp.maximum(_mm(h_ref[...], wu_ref[...]), 0.0)
    o_ref[...] += _mm((u * u).astype(BF16), wd_ref[...])

    @pl.when(j == pl.num_programs(1) - 1)
    def _():
        o_ref[...] = x_ref[...] + _rms(o_ref[...], gpost_ref[...])


def _mlp(x, gpre, wu, wd, gpost, l, tm, tf):
    n, d = x.shape
    f = wu.shape[-1]
    return pl.pallas_call(
        _mlp_kernel,
        grid=(n // tm, f // tf),
        in_specs=[pl.BlockSpec((tm, d), lambda i, j: (i, 0)),
                  pl.BlockSpec((None, 1, d), lambda i, j: (l, 0, 0)),
                  pl.BlockSpec((None, d, tf), lambda i, j: (l, 0, j)),
                  pl.BlockSpec((None, tf, d), lambda i, j: (l, j, 0)),
                  pl.BlockSpec((None, 1, d), lambda i, j: (l, 0, 0))],
        out_specs=pl.BlockSpec((tm, d), lambda i, j: (i, 0)),
        out_shape=jax.ShapeDtypeStruct((n, d), F32),
        scratch_shapes=[pltpu.VMEM((tm, d), BF16)],
        compiler_params=_cp("parallel", "arbitrary"),
        name="mlp",
    )(x, gpre, wu, wd, gpost)


def _col_tile(n, target):
    best = LANES
    for t in range(2 * LANES, target + 1, 2 * LANES):
        if n % t == 0:
            best = t
    return best


def _row_tile(n, target):
    t = min(n, target)
    while n % t:
        t //= 2
    return t


def kernel(x_prompt, x_sample, cache_k, cache_v, cache_logf, page_table, state_mlstm_C, state_mlstm_n, state_mlstm_m, state_ssd, state_conv, w_in, m_b_i, m_b_f, m_norm, f_b_f, s_conv_w, s_conv_b, s_dt_bias, s_A_log, s_D, s_norm, w_branch, w_out, ln_mix_pre, ln_mix_post, ln_mlp_pre, ln_mlp_post, w_up, w_down):
    bp, tp, d = x_prompt.shape
    bs, ts, _ = x_sample.shape
    depth = w_in.shape[0]
    mix = m_norm.shape[-1]
    mh, fh, sh = m_b_i.shape[-1], f_b_f.shape[-1], s_A_log.shape[-1]
    scd = s_conv_w.shape[-1]
    kconv = s_conv_w.shape[1]
    s_hd, ns = state_ssd.shape[-2], state_ssd.shape[-1]
    lay = _Layout(d, mix, mh, fh, sh, scd)
    assert w_in.shape[-1] == lay.in_width
    assert tp % CHUNK == 0 and ts <= SUBLANES
    n_pool, page = cache_k.shape[1], cache_k.shape[2]
    dk, dv, fhd = lay.dk, lay.dv, lay.fhd
    tsp = SUBLANES

    w_in_p, w_in_g = _pack_w_in(w_in, lay)
    wb = w_branch.astype(BF16)
    wo = w_out.astype(BF16)
    wu = w_up.astype(BF16)
    wd = w_down.astype(BF16)
    assert lay.small_order == ["ff", "mi", "mf", "sdt"]
    bias_small = jnp.concatenate([f_b_f, m_b_i, m_b_f, s_dt_bias,
                                  jnp.zeros((depth, LANES - lay.small_used), F32)], axis=-1)
    bias_row = bias_small[:, None, :]
    bias_col = bias_small[:, :, None]
    alog = jnp.zeros((depth, LANES), F32).at[:, lay.lane["sdt"]:lay.lane["sdt"] + sh].set(s_A_log)
    alog_row = alog[:, None, :]
    alog_col = alog[:, :, None]
    d_exp = jnp.repeat(s_D, s_hd, axis=-1)[:, None, :]
    r3 = lambda a: a[:, None, :]
    m_norm3, s_norm3, conv_b3 = r3(m_norm), r3(s_norm), r3(s_conv_b)
    g_mix_pre, g_mix_post, g_mlp_pre, g_mlp_post = r3(ln_mix_pre), r3(ln_mix_post), r3(ln_mlp_pre), r3(ln_mlp_post)

    lf_flat = cache_logf.reshape(depth * n_pool, page * fh)
    rt = _lfsuffix(lf_flat, fh, _row_tile(depth * n_pool, 512))[:, None, :]
    npg = _row_tile(page_table.shape[1], 16)

    np_, ns_ = bp * tp, bs * ts
    xp = x_prompt.reshape(np_, d)
    xs = x_sample.reshape(ns_, d)
    tm_p = _row_tile(np_, 1024)
    tm_s = _row_tile(ns_, 1024)
    tn_in = _col_tile(lay.width, 1536)
    tq = _row_tile(tp, 1024)

    zero_c = jnp.zeros((1, bp, mh, dv, dk), F32)
    zero_n = jnp.zeros((1, bp, mh, 1, dk), F32)
    zero_m = jnp.zeros((1, bp, mh, 1, 1), F32)
    zero_h = jnp.zeros((1, bp, sh * s_hd, ns), F32)
    c0_s = state_mlstm_C
    n0_s = state_mlstm_n[:, :, :, None, :]
    m0_s = state_mlstm_m[:, :, :, None, None]
    h0_s = state_ssd.reshape(depth, bs, sh * s_hd, ns)
    zero_cv = jnp.zeros((bp, SUBLANES, scd), F32)

    outs_p = [[] for _ in range(8)]
    outs_s = [[] for _ in range(8)]
    for l in range(depth):
        z, zg = _inproj(xp, g_mix_pre, w_in_p, w_in_g, l, tm_p, tn_in)
        z3 = z.reshape(bp, tp, lay.width)
        zg3 = zg.reshape(bp, tp, LANES)
        hm, c_p, n_p, m_p, lf_p, crow = _mlstm(z3, zg3, lay, CHUNK, bias_row, bias_col, m_norm3, l,
                                               zero_c, zero_n, zero_m, 0)
        hf = _flash(z3, crow, lay, tq)
        ysd, h_p = _ssd(z3, zg3, lay, CHUNK, bias_row, bias_col, alog_row, alog_col, s_conv_w, conv_b3, d_exp, s_norm3,
                        l, zero_h, 0, zero_cv, ns, s_hd)
        mixed = _merge(hm.reshape(np_, mix), hf.reshape(np_, mix), ysd.reshape(np_, mix), wb, z, lay, l, tm_p, 512)
        xp = _outproj(mixed, wo, g_mix_post, xp, l, _row_tile(np_, 512))
        xp = _mlp(xp, g_mlp_pre, wu, wd, g_mlp_post, l, tm_p, 512)
        fk_o, fv_o, u_o = lay.dst["fk"], lay.dst["fv"], lay.dst["sxbc"]
        for lst, a in zip(outs_p, (z3[:, :, fk_o:fk_o + mix].reshape(bp, tp, fh, fhd),
                                   z3[:, :, fv_o:fv_o + mix].reshape(bp, tp, fh, fhd),
                                   lf_p, c_p, n_p.reshape(bp, mh, dk), m_p.reshape(bp, mh),
                                   h_p.reshape(bp, sh, s_hd, ns), z3[:, tp - (kconv - 1):, u_o:u_o + scd])):
            lst.append(a)

        zs, zsg = _inproj(xs, g_mix_pre, w_in_p, w_in_g, l, tm_s, tn_in)
        zs3 = zs.reshape(bs, ts, lay.width)
        zs8 = jnp.pad(zs3, ((0, 0), (0, tsp - ts), (0, 0)))
        zsg8 = jnp.pad(zsg.reshape(bs, ts, LANES), ((0, 0), (0, tsp - ts), (0, 0)))
        hm_s, c_s, n_s, m_s, lf_s, crow_s = _mlstm(zs8, zsg8, lay, ts, bias_row, bias_col, m_norm3, l,
                                                   c0_s, n0_s, m0_s, l)
        fnew = jnp.transpose(crow_s[:, :, :ts], (0, 2, 1)).reshape(bs, 1, ts * fh)
        fnew = jnp.pad(fnew, ((0, 0), (0, 0), (0, LANES - ts * fh)))
        z5 = zs8.reshape(bs, tsp, lay.width // LANES, LANES)
        hf_s = _decode(z5, fnew, cache_k, cache_v, rt, page_table, lay, l, ts, npg)
        cv0 = jnp.pad(state_conv[l], ((0, 0), (SUBLANES - (kconv - 1), 0), (0, 0)))
        ys_s, h_s = _ssd(zs8, zsg8, lay, ts, bias_row, bias_col, alog_row, alog_col, s_conv_w, conv_b3, d_exp, s_norm3,
                         l, h0_s, l, cv0, ns, s_hd)
        mixed_s = _merge(hm_s[:, :ts].reshape(ns_, mix), hf_s.reshape(ns_, mix).astype(BF16),
                         ys_s[:, :ts].reshape(ns_, mix), wb, zs, lay, l, tm_s, 512)
        xs = _outproj(mixed_s, wo, g_mix_post, xs, l, tm_s)
        xs = _mlp(xs, g_mlp_pre, wu, wd, g_mlp_post, l, tm_s, 512)
        conv_full = jnp.concatenate([state_conv[l], zs3[:, :, u_o:u_o + scd]], axis=1)
        for lst, a in zip(outs_s, (zs3[:, :, fk_o:fk_o + mix].reshape(bs, ts, fh, fhd),
                                   zs3[:, :, fv_o:fv_o + mix].reshape(bs, ts, fh, fhd),
                                   lf_s[:, :ts], c_s, n_s.reshape(bs, mh, dk), m_s.reshape(bs, mh),
                                   h_s.reshape(bs, sh, s_hd, ns), conv_full[:, -(kconv - 1):])):
            lst.append(a)

    k_p, v_p, lf_pp, c_pp, n_pp, m_pp, ssd_p, conv_p = [jnp.stack(a) for a in outs_p]
    k_s, v_s, lf_ss, c_ss, n_ss, m_ss, ssd_s, conv_s = [jnp.stack(a) for a in outs_s]
    return (xp.reshape(bp, tp, d), xs.reshape(bs, ts, d), k_p, v_p, lf_pp, k_s, v_s, lf_ss,
            c_pp, n_pp, m_pp, c_ss, n_ss, m_ss, ssd_p, ssd_s, conv_p, conv_s)
```

```python
import functools

import jax
import jax.numpy as jnp
from jax import lax
from jax.experimental import pallas as pl
from jax.experimental.pallas import tpu as pltpu

F32 = jnp.float32
BF16 = jnp.bfloat16
HI = lax.Precision.HIGHEST
EPS = 1e-6
NEG_INF = float("-inf")
LOG2E = 1.4426950408889634
LANES = 128
SUBLANES = 8
CHUNK = 128
VMEM_LIMIT = 56 * 1024 * 1024


def _cp(*sem, vmem=VMEM_LIMIT):
    return pltpu.CompilerParams(dimension_semantics=sem, vmem_limit_bytes=vmem)


def _nt(a, b):
    return lax.dot_general(a, b, (((1,), (1,)), ((), ())), preferred_element_type=F32)


def _mm(a, b):
    return jnp.dot(a, b, preferred_element_type=F32)


def _mm_exact(a, b):
    return jnp.dot(a, b, preferred_element_type=F32, precision=HI)


def _softplus(x):
    return jnp.maximum(x, 0.0) + jnp.log1p(jnp.exp(-jnp.abs(x)))


def _log_sigmoid(x):
    return -_softplus(-x)


def _sigmoid(x):
    return 1.0 / (1.0 + jnp.exp(-x))


def _silu(x):
    return x * _sigmoid(x)


def _rms(x, g):
    return x * lax.rsqrt(jnp.mean(x * x, axis=-1, keepdims=True) + EPS) * g


def _pad_rows(a, rows):
    if a.shape[0] == rows:
        return a
    return jnp.concatenate([a, jnp.zeros((rows - a.shape[0], a.shape[1]), a.dtype)], axis=0)


def _imod(x, n):
    assert n & (n - 1) == 0
    return x & (n - 1)


def _idiv(x, n):
    assert n & (n - 1) == 0
    return x >> (n.bit_length() - 1)


def _pick_col(a, sel):
    return jnp.sum(jnp.where(sel, a, 0.0), axis=1, keepdims=True)


class _Layout:
    def __init__(self, d_model, mix, mh, fh, sh, scd):
        self.d, self.mix, self.mh, self.fh, self.sh, self.scd = d_model, mix, mh, fh, sh, scd
        self.dk = mix // mh // 2
        self.dv = mix // mh
        self.fhd = mix // fh
        src = [("mq", mh * self.dk), ("mk", mh * self.dk), ("mv", mix), ("mi", mh), ("mf", mh), ("mo", mix),
               ("fq", mix), ("fk", mix), ("fv", mix), ("ff", fh), ("sz", mix), ("sxbc", scd), ("sdt", sh),
               ("g", 3 * d_model)]
        self.src = {}
        off = 0
        for name, w in src:
            self.src[name] = (off, w)
            off += w
        self.in_width = off
        order = ["mq", "mk", "mv", "sxbc", "mo", "fq", "fk", "fv", "sz", "g"]
        self.dst = {}
        off = 0
        for name in order:
            self.dst[name] = off
            off += self.src[name][1]
        self.width = off
        self.small_order = ["ff", "mi", "mf", "sdt"]
        self.lane = {}
        l = 0
        for name in self.small_order:
            self.lane[name] = l
            l += self.src[name][1]
        assert l <= LANES
        self.small_used = l
        self.order = order


PACK_ROWS = 512


def _pack_kernel(tab_ref, w_ref, o_ref):
    o_ref[...] = w_ref[0].T.astype(BF16)


def _pack_w_in(w_in, lay):
    depth, d, _ = w_in.shape
    wt = jnp.swapaxes(w_in, 1, 2)
    starts = []
    for name in lay.order:
        src, width = lay.src[name]
        assert src % SUBLANES == 0 and width % PACK_ROWS == 0
        starts += [src + kblk * PACK_ROWS for kblk in range(width // PACK_ROWS)]
    table = jnp.asarray([s // SUBLANES for s in starts], jnp.int32)
    small = jnp.concatenate([w_in[:, :, lay.src[n][0]:lay.src[n][0] + lay.src[n][1]] for n in lay.small_order]
                            + [jnp.zeros((depth, d, LANES - lay.small_used), F32)], axis=2)
    grid_spec = pltpu.PrefetchScalarGridSpec(
        num_scalar_prefetch=1,
        grid=(depth, len(starts)),
        in_specs=[pl.BlockSpec((pl.Element(1), pl.Element(PACK_ROWS), pl.Element(d)),
                               lambda l, i, tab: (l, tab[i] * SUBLANES, 0))],
        out_specs=pl.BlockSpec((None, d, PACK_ROWS), lambda l, i, tab: (l, 0, i)),
    )
    big = pl.pallas_call(
        _pack_kernel,
        grid_spec=grid_spec,
        out_shape=jax.ShapeDtypeStruct((depth, d, lay.width), BF16),
        compiler_params=_cp("parallel", "parallel"),
        name="pack_w_in",
    )(table, wt)
    return big, small


def _inproj_kernel(x_ref, g_ref, w_ref, ws_ref, o_ref, os_ref, h_ref, *, n_big):
    j = pl.program_id(1)

    @pl.when(j == 0)
    def _():
        h_ref[...] = _rms(x_ref[...], g_ref[...]).astype(BF16)

    @pl.when(j < n_big)
    def _():
        o_ref[...] = _mm(h_ref[...], w_ref[...])

    @pl.when(j == n_big)
    def _():
        os_ref[...] = _mm(h_ref[...], ws_ref[...].astype(BF16))


def _inproj(x, g, w, ws, l, tm, tn):
    n, d = x.shape
    width = w.shape[-1]
    n_big = width // tn
    last = n_big - 1
    return pl.pallas_call(
        functools.partial(_inproj_kernel, n_big=n_big),
        grid=(n // tm, n_big + 1),
        in_specs=[pl.BlockSpec((tm, d), lambda i, j: (i, 0)),
                  pl.BlockSpec((None, 1, d), lambda i, j: (l, 0, 0)),
                  pl.BlockSpec((None, d, tn), lambda i, j: (l, 0, jnp.minimum(j, last))),
                  pl.BlockSpec((None, d, LANES), lambda i, j: (l, 0, 0))],
        out_specs=[pl.BlockSpec((tm, tn), lambda i, j: (i, jnp.minimum(j, last))),
                   pl.BlockSpec((tm, LANES), lambda i, j: (i, 0))],
        out_shape=[jax.ShapeDtypeStruct((n, width), F32), jax.ShapeDtypeStruct((n, LANES), F32)],
        scratch_shapes=[pltpu.VMEM((tm, d), BF16)],
        compiler_params=_cp("parallel", "arbitrary"),
        name="inproj",
    )(x, g, w, ws)


def _mlstm_kernel(q_ref, k_ref, v_ref, og_ref, s_ref, brow_ref, bcol_ref, mn_ref, c0_ref, n0_ref, m0_ref,
                  hm_ref, cout_ref, nout_ref, mout_ref, lf_ref, crow_ref, ct_ref, n_ref, m_ref, fcarry_ref, *,
                  L, lr, nh, dk, dv, lane_i, lane_f, lane_ff, nfh, kscale):
    c = pl.program_id(1)
    nc = pl.num_programs(1)

    @pl.when(c == 0)
    def _():
        for h in range(nh):
            ct_ref[h] = c0_ref[0, h].T
        n_ref[...] = n0_ref[0]
        m_ref[...] = m0_ref[0]
        fcarry_ref[...] = jnp.zeros_like(fcarry_ref)

    gates = _pad_rows(s_ref[0], L)
    row = lax.broadcasted_iota(jnp.int32, (L, 1), 0)
    col = lax.broadcasted_iota(jnp.int32, (1, L), 1)
    lane = lax.broadcasted_iota(jnp.int32, (1, LANES), 1)
    valid_c = row < lr
    valid_r = col < lr
    causal = col <= row
    gb = gates + brow_ref[...]
    gt = gates.T + bcol_ref[...]
    lf = jnp.where(valid_c, _log_sigmoid(gb), 0.0)
    lft = jnp.where(valid_r, _log_sigmoid(gt), 0.0)
    bc_all = _mm_exact(causal.astype(F32), lf)
    br_all = _mm_exact(lft, (row <= col).astype(F32))
    rows_out = hm_ref.shape[1]

    lf_ref[0] = lf[:rows_out, lane_ff:lane_ff + nfh]
    csum = br_all[lane_ff:lane_ff + nfh] + fcarry_ref[...]
    crow_ref[0] = csum
    fcarry_ref[...] = csum[:, L - 1:L]

    for h in range(nh):
        q = _pad_rows(q_ref[0, :, h * dk:(h + 1) * dk], L)
        k = _pad_rows(k_ref[0, :, h * dk:(h + 1) * dk], L) * kscale
        v = _pad_rows(v_ref[0, :, h * dv:(h + 1) * dv], L)
        i_c = jnp.where(valid_c, _pick_col(gb, lane == lane_i + h), NEG_INF)
        i_r = jnp.where(valid_r, gt[lane_i + h:lane_i + h + 1, :], NEG_INF)
        b_c = _pick_col(bc_all, lane == lane_f + h)
        b_r = br_all[lane_f + h:lane_f + h + 1, :]

        m_prev = m_ref[h]
        log_d = jnp.where(causal, b_c + (i_r - b_r), NEG_INF)
        inter = b_c + m_prev
        m_t = jnp.maximum(inter, jnp.max(log_d, axis=1, keepdims=True))
        w_inter = jnp.exp(inter - m_t)
        qb = q.astype(BF16)
        kb = k.astype(BF16)
        vb = v.astype(BF16)
        s = _nt(qb, kb) * jnp.exp(log_d - m_t)
        ct = ct_ref[h]
        nvec = n_ref[h]
        num = _mm(s.astype(BF16), vb) + w_inter * _mm(qb, ct.astype(BF16))
        nq = jnp.sum(qb.astype(F32) * nvec.astype(BF16).astype(F32), axis=1, keepdims=True)
        den = jnp.sum(s, axis=1, keepdims=True) + w_inter * nq
        hh = num / jnp.maximum(jnp.abs(den), jnp.exp(-m_t))

        b_last = b_c[L - 1:L, :]
        le_c = b_last - b_c + i_c
        le_r = b_last - b_r + i_r
        m_new = jnp.maximum(b_last + m_prev, jnp.max(le_r, axis=1, keepdims=True))
        we_c = jnp.exp(le_c - m_new)
        we_r = jnp.exp(le_r - m_new)
        ws = jnp.exp(b_last + m_prev - m_new)
        ct_ref[h] = ws * ct + _mm(k.T.astype(BF16), (we_c * v).astype(BF16))
        n_ref[h] = ws * nvec + _mm(jnp.broadcast_to(we_r, (SUBLANES, L)).astype(BF16), kb)[0:1]
        m_ref[h] = m_new

        y = _rms(hh, mn_ref[:, h * dv:(h + 1) * dv]) * _sigmoid(_pad_rows(og_ref[0, :, h * dv:(h + 1) * dv], L))
        hm_ref[0, :, h * dv:(h + 1) * dv] = y[:rows_out].astype(BF16)

    @pl.when(c == nc - 1)
    def _():
        for h in range(nh):
            cout_ref[0, h] = ct_ref[h].T
        nout_ref[0] = n_ref[...]
        mout_ref[0] = m_ref[...]


def _mlstm(z3, zg3, lay, lr, bias_row, bias_col, m_norm, l, c0, n0, m0, ls):
    bsz, t, _ = z3.shape
    rb = min(t, CHUNK)
    L = rb
    nc = t // rb
    mh, dk, dv = lay.mh, lay.dk, lay.dv
    fh = lay.fh
    kern = functools.partial(_mlstm_kernel, L=L, lr=lr, nh=mh, dk=dk, dv=dv, lane_i=lay.lane["mi"],
                             lane_f=lay.lane["mf"], lane_ff=lay.lane["ff"], nfh=fh, kscale=float(dk) ** -0.5)
    qw, vw = mh * dk, mh * dv
    qo, ko, vo, oo = lay.dst["mq"] // qw, lay.dst["mk"] // qw, lay.dst["mv"] // vw, lay.dst["mo"] // vw
    return pl.pallas_call(
        kern,
        grid=(bsz, nc),
        in_specs=[pl.BlockSpec((1, rb, qw), lambda b, c: (b, c, qo)),
                  pl.BlockSpec((1, rb, qw), lambda b, c: (b, c, ko)),
                  pl.BlockSpec((1, rb, vw), lambda b, c: (b, c, vo)),
                  pl.BlockSpec((1, rb, vw), lambda b, c: (b, c, oo)),
                  pl.BlockSpec((1, rb, LANES), lambda b, c: (b, c, 0)),
                  pl.BlockSpec((None, 1, LANES), lambda b, c: (l, 0, 0)),
                  pl.BlockSpec((None, LANES, 1), lambda b, c: (l, 0, 0)),
                  pl.BlockSpec((None, 1, vw), lambda b, c: (l, 0, 0)),
                  pl.BlockSpec((None, 1, mh, dv, dk), lambda b, c: (ls, b, 0, 0, 0)),
                  pl.BlockSpec((None, 1, mh, 1, dk), lambda b, c: (ls, b, 0, 0, 0)),
                  pl.BlockSpec((None, 1, mh, 1, 1), lambda b, c: (ls, b, 0, 0, 0))],
        out_specs=[pl.BlockSpec((1, rb, vw), lambda b, c: (b, c, 0)),
                   pl.BlockSpec((1, mh, dv, dk), lambda b, c: (b, 0, 0, 0)),
                   pl.BlockSpec((1, mh, 1, dk), lambda b, c: (b, 0, 0, 0)),
                   pl.BlockSpec((1, mh, 1, 1), lambda b, c: (b, 0, 0, 0)),
                   pl.BlockSpec((1, rb, fh), lambda b, c: (b, c, 0)),
                   pl.BlockSpec((1, fh, L), lambda b, c: (b, 0, c))],
        out_shape=[jax.ShapeDtypeStruct((bsz, t, vw), BF16),
                   jax.ShapeDtypeStruct((bsz, mh, dv, dk), F32),
                   jax.ShapeDtypeStruct((bsz, mh, 1, dk), F32),
                   jax.ShapeDtypeStruct((bsz, mh, 1, 1), F32),
                   jax.ShapeDtypeStruct((bsz, t, fh), F32),
                   jax.ShapeDtypeStruct((bsz, fh, nc * L), F32)],
        scratch_shapes=[pltpu.VMEM((mh, dk, dv), F32), pltpu.VMEM((mh, 1, dk), F32), pltpu.VMEM((mh, 1, 1), F32),
                        pltpu.VMEM((fh, 1), F32)],
        compiler_params=_cp("parallel", "arbitrary"),
        name="mlstm",
    )(z3, z3, z3, z3, zg3, bias_row, bias_col, m_norm, c0, n0, m0)


def _flash_kernel(q_ref, k_ref, v_ref, c_ref, o_ref, kb_ref, vb_ref, *, tq, scale):
    qi = pl.program_id(2)

    @pl.when(qi == 0)
    def _():
        kb_ref[...] = k_ref[0].astype(BF16)
        vb_ref[...] = v_ref[0].astype(BF16)

    q = (q_ref[0] * (scale * LOG2E)).astype(BF16)
    hd = q.shape[1]
    causal = lax.broadcasted_iota(jnp.int32, (1, tq), 1) <= lax.broadcasted_iota(jnp.int32, (tq, 1), 0)

    def block(j, carry, diagonal):
        m_prev, l_prev, acc = carry
        off = pl.multiple_of(j * tq, tq)
        s = _nt(q, kb_ref[pl.ds(off, tq), :]) - c_ref[0, 0, j] * LOG2E
        if diagonal:
            s = jnp.where(causal, s, NEG_INF)
        m_new = jnp.maximum(m_prev, jnp.max(s, axis=1, keepdims=True))
        alpha = jnp.exp2(m_prev - m_new)
        p = jnp.exp2(s - m_new)
        l_new = alpha * l_prev + jnp.sum(p, axis=1, keepdims=True)
        acc = alpha * acc + _mm(p.astype(BF16), vb_ref[pl.ds(off, tq), :])
        return m_new, l_new, acc

    init = (jnp.full((tq, 1), NEG_INF, F32), jnp.zeros((tq, 1), F32), jnp.zeros((tq, hd), F32))
    carry = lax.fori_loop(0, qi, lambda j, c: block(j, c, False), init)
    _, l_fin, acc = block(qi, carry, True)
    o_ref[0] = (acc / l_fin).astype(BF16)


def _flash(z3, crow, lay, tq):
    bsz, t, _ = z3.shape
    fh, hd = lay.fh, lay.fhd
    qo, ko, vo = lay.dst["fq"] // hd, lay.dst["fk"] // hd, lay.dst["fv"] // hd
    nq = t // tq
    c5 = crow.reshape(bsz, fh, nq, 1, tq)
    return pl.pallas_call(
        functools.partial(_flash_kernel, tq=tq, scale=float(hd) ** -0.5),
        grid=(bsz, fh, nq),
        in_specs=[pl.BlockSpec((1, tq, hd), lambda b, h, qi: (b, qi, qo + h)),
                  pl.BlockSpec((1, t, hd), lambda b, h, qi: (b, 0, ko + h)),
                  pl.BlockSpec((1, t, hd), lambda b, h, qi: (b, 0, vo + h)),
                  pl.BlockSpec((1, 1, nq, 1, tq), lambda b, h, qi: (b, h, 0, 0, 0))],
        out_specs=pl.BlockSpec((1, tq, hd), lambda b, h, qi: (b, qi, h)),
        out_shape=jax.ShapeDtypeStruct((bsz, t, fh * hd), BF16),
        scratch_shapes=[pltpu.VMEM((t, hd), BF16), pltpu.VMEM((t, hd), BF16)],
        compiler_params=_cp("parallel", "parallel", "arbitrary"),
        name="fox_flash",
    )(z3, z3, z3, c5)


def _lfsuffix_kernel(x_ref, o_ref, mat_ref, *, nh):
    w = x_ref.shape[1]

    @pl.when(pl.program_id(0) == 0)
    def _():
        ri = lax.broadcasted_iota(jnp.int32, (w, 1), 0)
        ci = lax.broadcasted_iota(jnp.int32, (1, w), 1)
        same = _imod(ri, nh) == _imod(ci, nh)
        mat_ref[:, 0:w] = jnp.where(same & (_idiv(ri, nh) > _idiv(ci, nh)), 1.0, 0.0).astype(BF16)
        mat_ref[:, w:2 * w] = jnp.where(same, 1.0, 0.0).astype(BF16)

    x = x_ref[...]
    x1 = x.astype(BF16)
    r1 = x - x1.astype(F32)
    x2 = r1.astype(BF16)
    x3 = (r1 - x2.astype(F32)).astype(BF16)
    mat = mat_ref[...]
    o_ref[...] = _mm(x1, mat) + (_mm(x2, mat) + _mm(x3, mat))


def _lfsuffix(lf_flat, nh, tm):
    n, w = lf_flat.shape
    return pl.pallas_call(
        functools.partial(_lfsuffix_kernel, nh=nh),
        grid=(n // tm,),
        in_specs=[pl.BlockSpec((tm, w), lambda i: (i, 0))],
        out_specs=pl.BlockSpec((tm, 2 * w), lambda i: (i, 0)),
        out_shape=jax.ShapeDtypeStruct((n, 2 * w), F32),
        scratch_shapes=[pltpu.VMEM((w, 2 * w), BF16)],
        compiler_params=_cp("arbitrary"),
        name="lf_suffix",
    )(lf_flat)


def _decode_kernel(pt_ref, q_ref, kn_ref, vn_ref, fn_ref, *refs, scale, nq, nh, npg):
    k_refs, v_refs, rt_refs = refs[0:npg], refs[npg:2 * npg], refs[2 * npg:3 * npg]
    o_ref, m_ref, l_ref, acc_ref, tail_ref = refs[3 * npg:]
    j = pl.program_id(1)
    nj = pl.num_programs(1)
    rows = nq * nh
    hd = q_ref.shape[-1]
    w = k_refs[0].shape[0] * k_refs[0].shape[1]

    @pl.when(j == 0)
    def _():
        m_ref[...] = jnp.full_like(m_ref, NEG_INF)
        l_ref[...] = jnp.zeros_like(l_ref)
        acc_ref[...] = jnp.zeros_like(acc_ref)
        tail_ref[...] = jnp.zeros_like(tail_ref)

    qs = q_ref[0].reshape(rows, hd).astype(BF16)
    row_h = _imod(lax.broadcasted_iota(jnp.int32, (rows, 1), 0), nh)

    def update(scores, values):
        m_prev = m_ref[...]
        smax = scores[0]
        for s in scores[1:]:
            smax = jnp.maximum(smax, s)
        m_new = jnp.maximum(m_prev, jnp.max(smax, axis=1, keepdims=True))
        alpha = jnp.exp(m_prev - m_new)
        psum = None
        pv = None
        for s, vb in zip(scores, values):
            p = jnp.exp(s - m_new)
            psum = p if psum is None else psum + p
            d = _mm(p.astype(BF16), vb)
            pv = d if pv is None else pv + d
        l_ref[...] = alpha * l_ref[...] + jnp.sum(psum, axis=1, keepdims=True)
        acc_ref[...] = alpha * acc_ref[...] + pv
        m_ref[...] = m_new

    same_head = _imod(lax.broadcasted_iota(jnp.int32, (1, w), 1), nh) == row_h
    tail = tail_ref[...]
    scores, values = [], []
    for i in range(npg):
        kf = k_refs[i][...].reshape(w, hd).astype(BF16)
        rt = rt_refs[i][0]
        s = _nt(qs, kf) * scale + (rt[:, 0:w] + tail)
        scores.append(jnp.where(same_head, s, NEG_INF))
        values.append(v_refs[i][...].reshape(w, hd).astype(BF16))
        tail = tail + rt[:, w:2 * w]
    tail_ref[...] = tail
    update(scores, values)

    @pl.when(j == nj - 1)
    def _():
        kn = _pad_rows(kn_ref[0].reshape(rows, hd), LANES).astype(BF16)
        vn = _pad_rows(vn_ref[0].reshape(rows, hd), LANES).astype(BF16)
        lane = lax.broadcasted_iota(jnp.int32, (1, LANES), 1)
        row_q = _idiv(lax.broadcasted_iota(jnp.int32, (rows, 1), 0), nh)
        ok = (_imod(lane, nh) == row_h) & (_idiv(lane, nh) <= row_q) & (lane < rows)
        s_new = _nt(qs, kn) * scale - fn_ref[0]
        update([jnp.where(ok, s_new, NEG_INF)], [vn])
        o_ref[0] = (acc_ref[...] / l_ref[...]).reshape(nq, nh, hd)


def _decode(z5, fnew, cache_k, cache_v, rt, page_table, lay, l, nq, npg):
    bsz = z5.shape[0]
    fh, hd = lay.fh, lay.fhd
    n_pool, page = cache_k.shape[1], cache_k.shape[2]
    n_pages = page_table.shape[1]
    w = page * fh
    qo, ko, vo = lay.dst["fq"] // (fh * hd), lay.dst["fk"] // (fh * hd), lay.dst["fv"] // (fh * hd)

    def kv_spec(i):
        return pl.BlockSpec((None, None, page, fh, hd),
                            lambda b, j, pt: (l, pt[b, n_pages - 1 - (j * npg + i)], 0, 0, 0))

    def rt_spec(i):
        return pl.BlockSpec((1, 1, 2 * w), lambda b, j, pt: (l * n_pool + pt[b, n_pages - 1 - (j * npg + i)], 0, 0))

    grid_spec = pltpu.PrefetchScalarGridSpec(
        num_scalar_prefetch=1,
        grid=(bsz, n_pages // npg),
        in_specs=[pl.BlockSpec((1, nq, fh, hd), lambda b, j, pt: (b, 0, qo, 0)),
                  pl.BlockSpec((1, nq, fh, hd), lambda b, j, pt: (b, 0, ko, 0)),
                  pl.BlockSpec((1, nq, fh, hd), lambda b, j, pt: (b, 0, vo, 0)),
                  pl.BlockSpec((1, 1, LANES), lambda b, j, pt: (b, 0, 0))]
                 + [kv_spec(i) for i in range(npg)] + [kv_spec(i) for i in range(npg)]
                 + [rt_spec(i) for i in range(npg)],
        out_specs=pl.BlockSpec((1, nq, fh, hd), lambda b, j, pt: (b, 0, 0, 0)),
        scratch_shapes=[pltpu.VMEM((nq * fh, 1), F32), pltpu.VMEM((nq * fh, 1), F32),
                        pltpu.VMEM((nq * fh, hd), F32), pltpu.VMEM((1, w), F32)],
    )
    return pl.pallas_call(
        functools.partial(_decode_kernel, scale=float(hd) ** -0.5, nq=nq, nh=fh, npg=npg),
        grid_spec=grid_spec,
        out_shape=jax.ShapeDtypeStruct((bsz, nq, fh, hd), F32),
        compiler_params=_cp("parallel", "arbitrary"),
        name="fox_decode",
    )(page_table, z5, z5, z5, fnew, *([cache_k] * npg), *([cache_v] * npg), *([rt] * npg))


def _ssd_kernel(u_ref, sz_ref, s_ref, brow_ref, bcol_ref, alrow_ref, alcol_ref, cw_ref, cb_ref, dexp_ref, sn_ref,
                h0_ref, cv0_ref, y_ref, hout_ref, xs_ref, ht_ref, *, L, lr, ng, hpg, hd, ns, lane_dt, kconv):
    c = pl.program_id(1)
    nc = pl.num_programs(1)
    mix = ng * hpg * hd
    gw = hpg * hd

    @pl.when(c == 0)
    def _():
        xs_ref[0:SUBLANES, :] = cv0_ref[0]
        for g in range(ng):
            ht_ref[g] = h0_ref[0, g * gw:(g + 1) * gw, :].T

    @pl.when(c > 0)
    def _():
        xs_ref[0:SUBLANES, :] = xs_ref[L:L + SUBLANES, :]

    xs_ref[SUBLANES:SUBLANES + L, :] = _pad_rows(u_ref[0], L)
    conv = cb_ref[...]
    for jj in range(kconv):
        off = SUBLANES - (kconv - 1) + jj
        conv = conv + cw_ref[jj:jj + 1, :] * xs_ref[pl.ds(off, L), :]
    xbc = _silu(conv)
    x = xbc[:, :mix]

    gates = _pad_rows(s_ref[0], L)
    row = lax.broadcasted_iota(jnp.int32, (L, 1), 0)
    col = lax.broadcasted_iota(jnp.int32, (1, L), 1)
    lane = lax.broadcasted_iota(jnp.int32, (1, LANES), 1)
    causal = col <= row
    dt_c = jnp.where(row < lr, _softplus(gates + brow_ref[...]), 0.0)
    dt_r = jnp.where(col < lr, _softplus(gates.T + bcol_ref[...]), 0.0)
    cum_c = _mm_exact(causal.astype(F32), dt_c * (-jnp.exp(alrow_ref[...])))
    cum_r = _mm_exact(dt_r * (-jnp.exp(alcol_ref[...])), (row <= col).astype(F32))
    lane_grp = _idiv(lax.broadcasted_iota(jnp.int32, (1, gw), 1), hd)

    ys = []
    for g in range(ng):
        bg = xbc[:, mix + g * ns:mix + (g + 1) * ns]
        cg = xbc[:, mix + (ng + g) * ns:mix + (ng + g + 1) * ns]
        bb = bg.astype(BF16)
        cb = cg.astype(BF16)
        cbm = _nt(cb, bb)
        xg = x[:, g * gw:(g + 1) * gw]
        ht = ht_ref[g]
        y = jnp.zeros((L, gw), F32)
        ecum = jnp.zeros((L, gw), F32)
        wend = jnp.zeros((L, gw), F32)
        dec = jnp.zeros((1, gw), F32)
        for r in range(hpg):
            idx = lane_dt + g * hpg + r
            cc = _pick_col(cum_c, lane == idx)
            dtc = _pick_col(dt_c, lane == idx)
            cr = cum_r[idx:idx + 1, :]
            dtr = dt_r[idx:idx + 1, :]
            wmat = cbm * jnp.exp(jnp.where(causal, cc - cr, NEG_INF)) * dtr
            sel = lane_grp == r
            y = y + _mm(wmat.astype(BF16), jnp.where(sel, xg, 0.0).astype(BF16))
            ce = cc[L - 1:L, :]
            ecum = jnp.where(sel, jnp.exp(cc), ecum)
            wend = jnp.where(sel, jnp.exp(ce - cc) * dtc, wend)
            dec = jnp.where(sel, jnp.exp(ce), dec)
        y = y + ecum * _mm(cb, ht.astype(BF16))
        ht_ref[g] = dec * ht + _mm(bg.T.astype(BF16), (wend * xg).astype(BF16))
        ys.append(y + dexp_ref[:, g * gw:(g + 1) * gw] * xg)
    yall = jnp.concatenate(ys, axis=1) * _silu(_pad_rows(sz_ref[0], L))
    y_ref[0] = _rms(yall, sn_ref[...])[:y_ref.shape[1]].astype(BF16)

    @pl.when(c == nc - 1)
    def _():
        for g in range(ng):
            hout_ref[0, g * gw:(g + 1) * gw, :] = ht_ref[g].T


def _ssd(z3, zg3, lay, lr, bias_row, bias_col, alog_row, alog_col, conv_w, conv_b, d_exp, s_norm, l, h0, ls, cv0, ns, hd):
    bsz, t, _ = z3.shape
    rb = min(t, CHUNK)
    L = rb
    nc = t // rb
    mix, scd, sh = lay.mix, lay.scd, lay.sh
    ng = (scd - mix) // (2 * ns)
    hpg = sh // ng
    kconv = conv_w.shape[1]
    kern = functools.partial(_ssd_kernel, L=L, lr=lr, ng=ng, hpg=hpg, hd=hd, ns=ns, lane_dt=lay.lane["sdt"],
                             kconv=kconv)
    uo, zo = lay.dst["sxbc"] // scd, lay.dst["sz"] // mix
    return pl.pallas_call(
        kern,
        grid=(bsz, nc),
        in_specs=[pl.BlockSpec((1, rb, scd), lambda b, c: (b, c, uo)),
                  pl.BlockSpec((1, rb, mix), lambda b, c: (b, c, zo)),
                  pl.BlockSpec((1, rb, LANES), lambda b, c: (b, c, 0)),
                  pl.BlockSpec((None, 1, LANES), lambda b, c: (l, 0, 0)),
                  pl.BlockSpec((None, LANES, 1), lambda b, c: (l, 0, 0)),
                  pl.BlockSpec((None, 1, LANES), lambda b, c: (l, 0, 0)),
                  pl.BlockSpec((None, LANES, 1), lambda b, c: (l, 0, 0)),
                  pl.BlockSpec((None, kconv, scd), lambda b, c: (l, 0, 0)),
                  pl.BlockSpec((None, 1, scd), lambda b, c: (l, 0, 0)),
                  pl.BlockSpec((None, 1, mix), lambda b, c: (l, 0, 0)),
                  pl.BlockSpec((None, 1, mix), lambda b, c: (l, 0, 0)),
                  pl.BlockSpec((None, 1, sh * hd, ns), lambda b, c: (ls, b, 0, 0)),
                  pl.BlockSpec((1, SUBLANES, scd), lambda b, c: (b, 0, 0))],
        out_specs=[pl.BlockSpec((1, rb, mix), lambda b, c: (b, c, 0)),
                   pl.BlockSpec((1, sh * hd, ns), lambda b, c: (b, 0, 0))],
        out_shape=[jax.ShapeDtypeStruct((bsz, t, mix), BF16),
                   jax.ShapeDtypeStruct((bsz, sh * hd, ns), F32)],
        scratch_shapes=[pltpu.VMEM((L + 2 * SUBLANES, scd), F32), pltpu.VMEM((ng, ns, hpg * hd), F32)],
        compiler_params=_cp("parallel", "arbitrary"),
        name="ssd",
    )(z3, z3, zg3, bias_row, bias_col, alog_row, alog_col, conv_w, conv_b, d_exp, s_norm, h0, cv0)


def _merge_kernel(hm_ref, hf_ref, ys_ref, wb_ref, g0_ref, g1_ref, g2_ref, o_ref):
    acc = _sigmoid(g0_ref[...]) * _mm(hm_ref[...], wb_ref[0])
    acc = acc + _sigmoid(g1_ref[...]) * _mm(hf_ref[...], wb_ref[1])
    acc = acc + _sigmoid(g2_ref[...]) * _mm(ys_ref[...], wb_ref[2])
    o_ref[...] = acc.astype(BF16)


def _merge(hm, hf, ys, wb, z, lay, l, tm, tn):
    n, mix = hm.shape
    d = lay.d
    go = lay.dst["g"] // tn
    per = d // tn
    gspec = lambda k: pl.BlockSpec((tm, tn), lambda i, j: (i, go + k * per + j))
    act = pl.BlockSpec((tm, mix), lambda i, j: (i, 0))
    return pl.pallas_call(
        _merge_kernel,
        grid=(n // tm, d // tn),
        in_specs=[act, act, act, pl.BlockSpec((None, 3, mix, tn), lambda i, j: (l, 0, 0, j)),
                  gspec(0), gspec(1), gspec(2)],
        out_specs=pl.BlockSpec((tm, tn), lambda i, j: (i, j)),
        out_shape=jax.ShapeDtypeStruct((n, d), BF16),
        compiler_params=_cp("parallel", "arbitrary"),
        name="merge",
    )(hm, hf, ys, wb, z, z, z)


def _outproj_kernel(a_ref, w_ref, g_ref, x_ref, o_ref):
    o_ref[...] = x_ref[...] + _rms(_mm(a_ref[...], w_ref[...]), g_ref[...])


def _outproj(a, w, g, x, l, tm):
    n, d = x.shape
    return pl.pallas_call(
        _outproj_kernel,
        grid=(n // tm,),
        in_specs=[pl.BlockSpec((tm, d), lambda i: (i, 0)),
                  pl.BlockSpec((None, d, d), lambda i: (l, 0, 0)),
                  pl.BlockSpec((None, 1, d), lambda i: (l, 0, 0)),
                  pl.BlockSpec((tm, d), lambda i: (i, 0))],
        out_specs=pl.BlockSpec((tm, d), lambda i: (i, 0)),
        out_shape=jax.ShapeDtypeStruct((n, d), F32),
        compiler_params=_cp("parallel"),
        name="outproj",
    )(a, w, g, x)


def _mlp_kernel(x_ref, gpre_ref, wu_ref, wd_ref, gpost_ref, o_ref, h_ref):
    j = pl.program_id(1)

    @pl.when(j == 0)
    def _():
        h_ref[...] = _rms(x_ref[...], gpre_ref[...]).astype(BF16)
        o_ref[...] = jnp.zeros_like(o_ref)

    u = jnp.maximum(_mm(h_ref[...], wu_ref[...]), 0.0)
    o_ref[...] += _mm((u * u).astype(BF16), wd_ref[...])

    @pl.when(j == pl.num_programs(1) - 1)
    def _():
        o_ref[...] = x_ref[...] + _rms(o_ref[...], gpost_ref[...])


def _mlp(x, gpre, wu, wd, gpost, l, tm, tf):
    n, d = x.shape
    f = wu.shape[-1]
    return pl.pallas_call(
        _mlp_kernel,
        grid=(n // tm, f // tf),
        in_specs=[pl.BlockSpec((tm, d), lambda i, j: (i, 0)),
                  pl.BlockSpec((None, 1, d), lambda i, j: (l, 0, 0)),
                  pl.BlockSpec((None, d, tf), lambda i, j: (l, 0, j)),
                  pl.BlockSpec((None, tf, d), lambda i, j: (l, j, 0)),
                  pl.BlockSpec((None, 1, d), lambda i, j: (l, 0, 0))],
        out_specs=pl.BlockSpec((tm, d), lambda i, j: (i, 0)),
        out_shape=jax.ShapeDtypeStruct((n, d), F32),
        scratch_shapes=[pltpu.VMEM((tm, d), BF16)],
        compiler_params=_cp("parallel", "arbitrary"),
        name="mlp",
    )(x, gpre, wu, wd, gpost)


def _col_tile(n, target):
    best = LANES
    for t in range(2 * LANES, target + 1, 2 * LANES):
        if n % t == 0:
            best = t
    return best


def _row_tile(n, target):
    t = min(n, target)
    while n % t:
        t //= 2
    return t


def kernel(x_prompt, x_sample, cache_k, cache_v, cache_logf, page_table, state_mlstm_C, state_mlstm_n, state_mlstm_m, state_ssd, state_conv, w_in, m_b_i, m_b_f, m_norm, f_b_f, s_conv_w, s_conv_b, s_dt_bias, s_A_log, s_D, s_norm, w_branch, w_out, ln_mix_pre, ln_mix_post, ln_mlp_pre, ln_mlp_post, w_up, w_down):
    bp, tp, d = x_prompt.shape
    bs, ts, _ = x_sample.shape
    depth = w_in.shape[0]
    mix = m_norm.shape[-1]
    mh, fh, sh = m_b_i.shape[-1], f_b_f.shape[-1], s_A_log.shape[-1]
    scd = s_conv_w.shape[-1]
    kconv = s_conv_w.shape[1]
    s_hd, ns = state_ssd.shape[-2], state_ssd.shape[-1]
    lay = _Layout(d, mix, mh, fh, sh, scd)
    assert w_in.shape[-1] == lay.in_width
    assert tp % CHUNK == 0 and ts <= SUBLANES
    n_pool, page = cache_k.shape[1], cache_k.shape[2]
    dk, dv, fhd = lay.dk, lay.dv, lay.fhd
    tsp = SUBLANES

    w_in_p, w_in_g = _pack_w_in(w_in, lay)
    wb = w_branch.astype(BF16)
    wo = w_out.astype(BF16)
    wu = w_up.astype(BF16)
    wd = w_down.astype(BF16)
    assert lay.small_order == ["ff", "mi", "mf", "sdt"]
    bias_small = jnp.concatenate([f_b_f, m_b_i, m_b_f, s_dt_bias,
                                  jnp.zeros((depth, LANES - lay.small_used), F32)], axis=-1)
    bias_row = bias_small[:, None, :]
    bias_col = bias_small[:, :, None]
    alog = jnp.zeros((depth, LANES), F32).at[:, lay.lane["sdt"]:lay.lane["sdt"] + sh].set(s_A_log)
    alog_row = alog[:, None, :]
    alog_col = alog[:, :, None]
    d_exp = jnp.repeat(s_D, s_hd, axis=-1)[:, None, :]
    r3 = lambda a: a[:, None, :]
    m_norm3, s_norm3, conv_b3 = r3(m_norm), r3(s_norm), r3(s_conv_b)
    g_mix_pre, g_mix_post, g_mlp_pre, g_mlp_post = r3(ln_mix_pre), r3(ln_mix_post), r3(ln_mlp_pre), r3(ln_mlp_post)

    lf_flat = cache_logf.reshape(depth * n_pool, page * fh)
    rt = _lfsuffix(lf_flat, fh, _row_tile(depth * n_pool, 512))[:, None, :]
    npg = _row_tile(page_table.shape[1], 16)

    np_, ns_ = bp * tp, bs * ts
    xp = x_prompt.reshape(np_, d)
    xs = x_sample.reshape(ns_, d)
    tm_p = _row_tile(np_, 1024)
    tm_s = _row_tile(ns_, 1024)
    tn_in = _col_tile(lay.width, 1536)
    tq = _row_tile(tp, 1024)

    zero_c = jnp.zeros((1, bp, mh, dv, dk), F32)
    zero_n = jnp.zeros((1, bp, mh, 1, dk), F32)
    zero_m = jnp.zeros((1, bp, mh, 1, 1), F32)
    zero_h = jnp.zeros((1, bp, sh * s_hd, ns), F32)
    c0_s = state_mlstm_C
    n0_s = state_mlstm_n[:, :, :, None, :]
    m0_s = state_mlstm_m[:, :, :, None, None]
    h0_s = state_ssd.reshape(depth, bs, sh * s_hd, ns)
    zero_cv = jnp.zeros((bp, SUBLANES, scd), F32)

    outs_p = [[] for _ in range(8)]
    outs_s = [[] for _ in range(8)]
    for l in range(depth):
        z, zg = _inproj(xp, g_mix_pre, w_in_p, w_in_g, l, tm_p, tn_in)
        z3 = z.reshape(bp, tp, lay.width)
        zg3 = zg.reshape(bp, tp, LANES)
        hm, c_p, n_p, m_p, lf_p, crow = _mlstm(z3, zg3, lay, CHUNK, bias_row, bias_col, m_norm3, l,
                                               zero_c, zero_n, zero_m, 0)
        hf = _flash(z3, crow, lay, tq)
        ysd, h_p = _ssd(z3, zg3, lay, CHUNK, bias_row, bias_col, alog_row, alog_col, s_conv_w, conv_b3, d_exp, s_norm3,
                        l, zero_h, 0, zero_cv, ns, s_hd)
        mixed = _merge(hm.reshape(np_, mix), hf.reshape(np_, mix), ysd.reshape(np_, mix), wb, z, lay, l, tm_p, 512)
        xp = _outproj(mixed, wo, g_mix_post, xp, l, _row_tile(np_, 512))
        xp = _mlp(xp, g_mlp_pre, wu, wd, g_mlp_post, l, tm_p, 512)
        fk_o, fv_o, u_o = lay.dst["fk"], lay.dst["fv"], lay.dst["sxbc"]
        for lst, a in zip(outs_p, (z3[:, :, fk_o:fk_o + mix].reshape(bp, tp, fh, fhd),
                                   z3[:, :, fv_o:fv_o + mix].reshape(bp, tp, fh, fhd),
                                   lf_p, c_p, n_p.reshape(bp, mh, dk), m_p.reshape(bp, mh),
                                   h_p.reshape(bp, sh, s_hd, ns), z3[:, tp - (kconv - 1):, u_o:u_o + scd])):
            lst.append(a)

        zs, zsg = _inproj(xs, g_mix_pre, w_in_p, w_in_g, l, tm_s, tn_in)
        zs3 = zs.reshape(bs, ts, lay.width)
        zs8 = jnp.pad(zs3, ((0, 0), (0, tsp - ts), (0, 0)))
        zsg8 = jnp.pad(zsg.reshape(bs, ts, LANES), ((0, 0), (0, tsp - ts), (0, 0)))
        hm_s, c_s, n_s, m_s, lf_s, crow_s = _mlstm(zs8, zsg8, lay, ts, bias_row, bias_col, m_norm3, l,
                                                   c0_s, n0_s, m0_s, l)
        fnew = jnp.transpose(crow_s[:, :, :ts], (0, 2, 1)).reshape(bs, 1, ts * fh)
        fnew = jnp.pad(fnew, ((0, 0), (0, 0), (0, LANES - ts * fh)))
        z5 = zs8.reshape(bs, tsp, lay.width // LANES, LANES)
        hf_s = _decode(z5, fnew, cache_k, cache_v, rt, page_table, lay, l, ts, npg)
        cv0 = jnp.pad(state_conv[l], ((0, 0), (SUBLANES - (kconv - 1), 0), (0, 0)))
        ys_s, h_s = _ssd(zs8, zsg8, lay, ts, bias_row, bias_col, alog_row, alog_col, s_conv_w, conv_b3, d_exp, s_norm3,
                         l, h0_s, l, cv0, ns, s_hd)
        mixed_s = _merge(hm_s[:, :ts].reshape(ns_, mix), hf_s.reshape(ns_, mix).astype(BF16),
                         ys_s[:, :ts].reshape(ns_, mix), wb, zs, lay, l, tm_s, 512)
        xs = _outproj(mixed_s, wo, g_mix_post, xs, l, tm_s)
        xs = _mlp(xs, g_mlp_pre, wu, wd, g_mlp_post, l, tm_s, 512)
        conv_full = jnp.concatenate([state_conv[l], zs3[:, :, u_o:u_o + scd]], axis=1)
        for lst, a in zip(outs_s, (zs3[:, :, fk_o:fk_o + mix].reshape(bs, ts, fh, fhd),
                                   zs3[:, :, fv_o:fv_o + mix].reshape(bs, ts, fh, fhd),
                                   lf_s[:, :ts], c_s, n_s.reshape(bs, mh, dk), m_s.reshape(bs, mh),
                                   h_s.reshape(bs, sh, s_hd, ns), conv_full[:, -(kconv - 1):])):
            lst.append(a)

    k_p, v_p, lf_pp, c_pp, n_pp, m_pp, ssd_p, conv_p = [jnp.stack(a) for a in outs_p]
    k_s, v_s, lf_ss, c_ss, n_ss, m_ss, ssd_s, conv_s = [jnp.stack(a) for a in outs_s]
    return (xp.reshape(bp, tp, d), xs.reshape(bs, ts, d), k_p, v_p, lf_pp, k_s, v_s, lf_ss,
            c_pp, n_pp, m_pp, c_ss, n_ss, m_ss, ssd_p, ssd_s, conv_p, conv_s)
```
